```python
import math
import jax, jax.numpy as jnp
from jax import lax
import numpy as np

D_MODEL = 1024
BATCH = 8
SEQ = 8192
DEPTH = 1
DEC_BATCH = 8
DEC_SEQ = 32
PAST_LEN = 1024

CHUNK = 64
D_MIX = D_MODEL
W_A = D_MIX // 2
W_B = D_MIX - W_A
LRU_BLOCKS = 8
LRU_BW = W_A // LRU_BLOCKS
CONV_W = 4
LRU_C = 8.0
H_B = 4
DK = W_B // H_B
DV = W_B // H_B
N_ADA = 6
PEER_HEADS = 8
N_KEYS = 128
N_EXPERTS = N_KEYS * N_KEYS
D_QUERY = 256
PEER_TOPK = 16
PEER_BLOCK = 128
EPS = 1e-6

OFF_XA = 0
OFF_GA = OFF_XA + W_A
OFF_Q = OFF_GA + W_A
OFF_K = OFF_Q + W_B
OFF_V = OFF_K + W_B
OFF_O = OFF_V + W_B
OFF_I = OFF_O + W_B
OFF_F = OFF_I + H_B
N_IN = OFF_F + H_B

kernel_name = 'hybrid_rglru_mlstm_peer_stream'


def rmsnorm(x, w):
    xf = x.astype(jnp.float32)
    y = xf * lax.rsqrt(jnp.mean(xf * xf, axis=-1, keepdims=True) + EPS)
    return (y * w.astype(jnp.float32)).astype(x.dtype)


def causal_conv(x, buf, w, b):
    T = x.shape[1]
    xp = jnp.concatenate([buf.astype(x.dtype), x], axis=1)
    y = b + xp[:, 0:T] * w[0]
    for k in range(1, CONV_W):
        y = y + xp[:, k:k + T] * w[k]
    return y, xp[:, xp.shape[1] - (CONV_W - 1):]


def _lin_combine(left, right):
    a_l, b_l = left
    a_r, b_r = right
    return a_l * a_r, a_r * b_l + b_r


def rglru(xa, ga, h0, conv0, conv_w, conv_b, wr, br, wi, bi, lam):
    B, T, _ = xa.shape
    u, conv_new = causal_conv(xa, conv0, conv_w, conv_b)
    uf = u.astype(jnp.float32)
    ub = uf.reshape(B, T, LRU_BLOCKS, LRU_BW)
    r = jax.nn.sigmoid(jnp.einsum('btnc,ncd->btnd', ub, wr.astype(jnp.float32)).reshape(B, T, W_A) + br)
    i = jax.nn.sigmoid(jnp.einsum('btnc,ncd->btnd', ub, wi.astype(jnp.float32)).reshape(B, T, W_A) + bi)
    log_a = -LRU_C * r * jax.nn.softplus(-lam.astype(jnp.float32))
    a = jnp.exp(log_a)
    bx = jnp.sqrt(-jnp.expm1(2.0 * log_a)) * (i * uf)
    a_cum, b_cum = lax.associative_scan(_lin_combine, (a, bx), axis=1)
    h = b_cum + a_cum * h0.astype(jnp.float32)[:, None, :]
    y = h * jax.nn.gelu(ga.astype(jnp.float32))
    return y.astype(xa.dtype), h[:, -1], conv_new


def mlstm_chunk(carry, inp):
    C, n, m = carry
    q, k, v, ig, lf = inp
    L = q.shape[2]
    b = jnp.cumsum(lf, axis=-1)
    causal = jnp.tril(jnp.ones((L, L), dtype=bool))
    logD = jnp.where(causal, b[..., :, None] - b[..., None, :] + ig[..., None, :], -jnp.inf)
    inter = b + m[..., None]
    m_t = jnp.maximum(inter, jnp.max(logD, axis=-1))
    D = jnp.exp(logD - m_t[..., None])
    w_state = jnp.exp(inter - m_t)
    S = jnp.einsum('bhtd,bhsd->bhts', q, k) * D
    num = jnp.einsum('bhts,bhse->bhte', S, v) + w_state[..., None] * jnp.einsum('bhtd,bhde->bhte', q, C)
    den = jnp.sum(S, axis=-1) + w_state * jnp.einsum('bhtd,bhd->bht', q, n)
    h = num / jnp.maximum(jnp.abs(den), jnp.exp(-m_t))[..., None]
    m_new = m_t[..., -1]
    w_k = jnp.exp(b[..., -1:] - b + ig - m_new[..., None])
    w_c = jnp.exp(inter[..., -1] - m_new)
    C_new = w_c[..., None, None] * C + jnp.einsum('bhs,bhsd,bhse->bhde', w_k, k, v)
    n_new = w_c[..., None] * n + jnp.einsum('bhs,bhsd->bhd', w_k, k)
    return (C_new, n_new, m_new), h


def mlstm(q, k, v, o, ig, fg, C0, n0, m0, norm_w, chunk_len):
    B, T, _ = q.shape

    def to_heads(t):
        return t.astype(jnp.float32).reshape(B, T, H_B, -1).transpose(0, 2, 1, 3)

    qh = to_heads(q) * (DK ** -0.5)
    kh = to_heads(k)
    vh = to_heads(v)
    igh = ig.astype(jnp.float32).transpose(0, 2, 1)
    lfh = jax.nn.log_sigmoid(fg.astype(jnp.float32)).transpose(0, 2, 1)
    nc = T // chunk_len

    def chunkify(t):
        return jnp.moveaxis(t.reshape((B, H_B, nc, chunk_len) + t.shape[3:]), 2, 0)

    carry0 = (C0.astype(jnp.float32), n0.astype(jnp.float32), m0.astype(jnp.float32))
    (C, n, m), hs = lax.scan(mlstm_chunk, carry0,
                             (chunkify(qh), chunkify(kh), chunkify(vh), chunkify(igh), chunkify(lfh)))
    h = jnp.moveaxis(hs, 0, 2).reshape(B, H_B, T, DV)
    h = h * lax.rsqrt(jnp.mean(h * h, axis=-1, keepdims=True) + EPS)
    h = h.transpose(0, 2, 1, 3).reshape(B, T, W_B) * norm_w.astype(jnp.float32)
    y = jax.nn.sigmoid(o.astype(jnp.float32)) * h
    return y.astype(q.dtype), C, n, m


def peer(h, wq, k1, k2, u, v):
    B, T, D = h.shape
    n_tok = B * T
    tok = h.reshape(n_tok, D)
    pad = (-n_tok) % PEER_BLOCK
    blocks = jnp.pad(tok, ((0, pad), (0, 0))).reshape(-1, PEER_BLOCK, D)

    def block_fn(xb):
        q = (xb @ wq).astype(jnp.float32).reshape(PEER_BLOCK, PEER_HEADS, D_QUERY)
        q = q * lax.rsqrt(jnp.mean(q * q, axis=-1, keepdims=True) + EPS)
        q1 = q[..., :D_QUERY // 2]
        q2 = q[..., D_QUERY // 2:]
        s1 = jnp.einsum('phd,kd->phk', q1, k1.astype(jnp.float32))
        s2 = jnp.einsum('phd,kd->phk', q2, k2.astype(jnp.float32))
        v1, i1 = lax.top_k(s1, PEER_TOPK)
        v2, i2 = lax.top_k(s2, PEER_TOPK)
        cand = (v1[..., :, None] + v2[..., None, :]).reshape(PEER_BLOCK, PEER_HEADS, PEER_TOPK * PEER_TOPK)
        sv, si = lax.top_k(cand, PEER_TOPK)
        e1 = jnp.take_along_axis(i1, si // PEER_TOPK, axis=-1)
        e2 = jnp.take_along_axis(i2, si % PEER_TOPK, axis=-1)
        expert = e1 * N_KEYS + e2
        g = jax.nn.softmax(sv, axis=-1)
        ue = jnp.take(u, expert, axis=0)
        act = jax.nn.gelu(jnp.einsum('phkd,pd->phk', ue, xb).astype(jnp.float32))
        ve = jnp.take(v, expert, axis=0)
        return jnp.einsum('phk,phkd->pd', (g * act).astype(xb.dtype), ve)

    out = lax.map(block_fn, blocks).reshape(-1, D)[:n_tok]
    return out.reshape(B, T, D)


def trunk_layer(x, c, h0, conv0, C0, n0, m0, chunk_len,
                norm1_w, ada_w, ada_b, w_in, b_in, conv_w, conv_b, lru_wr, lru_br, lru_wi, lru_bi,
                lru_lambda, mlstm_norm_w, w_out, norm2_w, peer_wq, peer_k1, peer_k2, peer_u, peer_v):
    B = x.shape[0]
    mod = (jax.nn.silu(c) @ ada_w + ada_b).reshape(B, N_ADA, 1, D_MODEL)
    sh1, sc1, g1, sh2, sc2, g2 = (mod[:, i] for i in range(N_ADA))
    h = rmsnorm(x, norm1_w) * (1.0 + sc1) + sh1
    z = h @ w_in + b_in
    ya, h_last, conv_new = rglru(z[..., OFF_XA:OFF_GA], z[..., OFF_GA:OFF_Q], h0, conv0,
                                 conv_w, conv_b, lru_wr, lru_br, lru_wi, lru_bi, lru_lambda)
    yb, C, n, m = mlstm(z[..., OFF_Q:OFF_K], z[..., OFF_K:OFF_V], z[..., OFF_V:OFF_O], z[..., OFF_O:OFF_I],
                        z[..., OFF_I:OFF_F], z[..., OFF_F:N_IN], C0, n0, m0, mlstm_norm_w, chunk_len)
    x = x + g1 * (jnp.concatenate([ya, yb], axis=-1) @ w_out)
    h2 = rmsnorm(x, norm2_w) * (1.0 + sc2) + sh2
    x = x + g2 * peer(h2, peer_wq, peer_k1, peer_k2, peer_u, peer_v)
    return x, h_last, conv_new, C, n, m


def setup_inputs(seed: int = 0) -> dict:
    key = jax.random.key(seed)
    ks = jax.random.split(key, 32)
    f32 = jnp.float32

    def nrm(k, shape, s):
        return s * jax.random.normal(k, shape, f32)

    a_c = jax.random.uniform(ks[21], (DEPTH, W_A), f32, 0.9, 0.999)
    sig = a_c ** (1.0 / LRU_C)
    b_in = nrm(ks[13], (DEPTH, N_IN), 0.01).at[:, OFF_F:OFF_F + H_B].add(
        jax.random.uniform(ks[14], (DEPTH, H_B), f32, 3.0, 6.0))
    return {
        'x_prompt': nrm(ks[0], (BATCH, SEQ, D_MODEL), 1.0),
        'x_sample': nrm(ks[1], (DEC_BATCH, DEC_SEQ, D_MODEL), 1.0),
        'c_prompt': nrm(ks[2], (BATCH, D_MODEL), 1.0),
        'c_sample': nrm(ks[3], (DEC_BATCH, D_MODEL), 1.0),
        'state_lru_h': nrm(ks[4], (DEPTH, DEC_BATCH, W_A), 0.5),
        'state_lru_conv': nrm(ks[5], (DEPTH, DEC_BATCH, CONV_W - 1, W_A), 1.0),
        'state_mlstm_C': nrm(ks[6], (DEPTH, DEC_BATCH, H_B, DK, DV), 0.1),
        'state_mlstm_n': nrm(ks[7], (DEPTH, DEC_BATCH, H_B, DK), 0.5),
        'state_mlstm_m': nrm(ks[8], (DEPTH, DEC_BATCH, H_B), 1.0),
        'norm1_w': 1.0 + nrm(ks[9], (DEPTH, D_MODEL), 0.01),
        'ada_w': nrm(ks[10], (DEPTH, D_MODEL, N_ADA * D_MODEL), 0.5 * D_MODEL ** -0.5),
        'ada_b': nrm(ks[11], (DEPTH, N_ADA * D_MODEL), 0.01),
        'w_in': nrm(ks[12], (DEPTH, D_MODEL, N_IN), D_MODEL ** -0.5),
        'b_in': b_in,
        'conv_w': nrm(ks[15], (DEPTH, CONV_W, W_A), CONV_W ** -0.5),
        'conv_b': nrm(ks[16], (DEPTH, W_A), 0.01),
        'lru_wr': nrm(ks[17], (DEPTH, LRU_BLOCKS, LRU_BW, LRU_BW), LRU_BW ** -0.5),
        'lru_br': nrm(ks[18], (DEPTH, W_A), 0.01),
        'lru_wi': nrm(ks[19], (DEPTH, LRU_BLOCKS, LRU_BW, LRU_BW), LRU_BW ** -0.5),
        'lru_bi': nrm(ks[20], (DEPTH, W_A), 0.01),
        'lru_lambda': jnp.log(sig) - jnp.log1p(-sig),
        'mlstm_norm_w': 1.0 + nrm(ks[22], (DEPTH, W_B), 0.01),
        'w_out': nrm(ks[23], (DEPTH, D_MIX, D_MODEL), D_MIX ** -0.5),
        'norm2_w': 1.0 + nrm(ks[24], (DEPTH, D_MODEL), 0.01),
        'peer_wq': nrm(ks[25], (DEPTH, D_MODEL, PEER_HEADS * D_QUERY), D_MODEL ** -0.5),
        'peer_k1': nrm(ks[26], (DEPTH, N_KEYS, D_QUERY // 2), (D_QUERY // 2) ** -0.5),
        'peer_k2': nrm(ks[27], (DEPTH, N_KEYS, D_QUERY // 2), (D_QUERY // 2) ** -0.5),
        'peer_u': nrm(ks[28], (DEPTH, N_EXPERTS, D_MODEL), D_MODEL ** -0.5),
        'peer_v': nrm(ks[29], (DEPTH, N_EXPERTS, D_MODEL), 1.0),
        'final_norm_w': 1.0 + nrm(ks[30], (D_MODEL,), 0.01),
    }


def reference(x_prompt, x_sample, c_prompt, c_sample, state_lru_h, state_lru_conv, state_mlstm_C,
              state_mlstm_n, state_mlstm_m, norm1_w, ada_w, ada_b, w_in, b_in, conv_w, conv_b,
              lru_wr, lru_br, lru_wi, lru_bi, lru_lambda, mlstm_norm_w, w_out, norm2_w,
              peer_wq, peer_k1, peer_k2, peer_u, peer_v, final_norm_w):
    B = x_prompt.shape[0]
    dt = x_prompt.dtype
    xp = x_prompt
    xs = x_sample
    new_p = [[], [], [], [], []]
    new_s = [[], [], [], [], []]
    for l in range(DEPTH):
        w_l = (norm1_w[l], ada_w[l], ada_b[l], w_in[l], b_in[l], conv_w[l], conv_b[l],
               lru_wr[l], lru_br[l], lru_wi[l], lru_bi[l], lru_lambda[l], mlstm_norm_w[l],
               w_out[l], norm2_w[l], peer_wq[l], peer_k1[l], peer_k2[l], peer_u[l], peer_v[l])
        xp, *st_p = trunk_layer(xp, c_prompt,
                                jnp.zeros((B, W_A), dt), jnp.zeros((B, CONV_W - 1, W_A), dt),
                                jnp.zeros((B, H_B, DK, DV), dt), jnp.zeros((B, H_B, DK), dt),
                                jnp.zeros((B, H_B), dt), CHUNK, *w_l)
        xs, *st_s = trunk_layer(xs, c_sample, state_lru_h[l], state_lru_conv[l], state_mlstm_C[l],
                                state_mlstm_n[l], state_mlstm_m[l], xs.shape[1], *w_l)
        for j in range(5):
            new_p[j].append(st_p[j])
            new_s[j].append(st_s[j])
    y_prompt = rmsnorm(xp, final_norm_w)
    y_sample = rmsnorm(xs, final_norm_w)
    p_h = jnp.stack(new_p[0]).astype(state_lru_h.dtype)
    p_conv = jnp.stack(new_p[1]).astype(state_lru_conv.dtype)
    p_C = jnp.stack(new_p[2]).astype(state_mlstm_C.dtype)
    p_n = jnp.stack(new_p[3]).astype(state_mlstm_n.dtype)
    p_m = jnp.stack(new_p[4]).astype(state_mlstm_m.dtype)
    s_h = jnp.stack(new_s[0]).astype(state_lru_h.dtype)
    s_conv = jnp.stack(new_s[1]).astype(state_lru_conv.dtype)
    s_C = jnp.stack(new_s[2]).astype(state_mlstm_C.dtype)
    s_n = jnp.stack(new_s[3]).astype(state_mlstm_n.dtype)
    s_m = jnp.stack(new_s[4]).astype(state_mlstm_m.dtype)
    return (y_prompt, y_sample, p_h, p_conv, p_C, p_n, p_m, s_h, s_conv, s_C, s_n, s_m)
```

```python
import functools

import jax
import jax.numpy as jnp
from jax import lax
from jax.experimental import pallas as pl
from jax.experimental.pallas import tpu as pltpu

D_MODEL = 1024
W_A = 512
W_B = 512
LRU_BLOCKS = 8
CONV_W = 4
LRU_C = 8.0
H_B = 4
DK = W_B // H_B
N_ADA = 6
PEER_HEADS = 8
N_KEYS = 128
D_QUERY = 256
PEER_TOPK = 16
N_PICK = PEER_HEADS * PEER_TOPK
EPS = 1e-6
CHUNK = 64
N_MAIN = 2 * W_A + 4 * W_B
N_GATE = 2 * H_B

LANES = 128
SUBLANES = 8
PEER_VMEM_LIMIT = 48 * 1024 * 1024

F32 = jnp.float32
BF16 = jnp.bfloat16
HIGHEST = lax.Precision.HIGHEST


def _rms(x, w):
    return x * lax.rsqrt(jnp.mean(x * x, axis=-1, keepdims=True) + EPS) * w


def _adaln_kernel(c_ref, w_ref, b_ref, o_ref):
    s = jax.nn.silu(c_ref[...])
    o_ref[...] = jnp.dot(s, w_ref[...], preferred_element_type=F32, precision=HIGHEST) + b_ref[...]


def _adaln(c, w, b):
    nb = c.shape[0]
    n_out = w.shape[1]
    tn = 1024
    return pl.pallas_call(
        _adaln_kernel,
        grid=(n_out // tn,),
        in_specs=[pl.BlockSpec((nb, D_MODEL), lambda j: (0, 0)),
                  pl.BlockSpec((D_MODEL, tn), lambda j: (0, j)),
                  pl.BlockSpec((1, tn), lambda j: (0, j))],
        out_specs=pl.BlockSpec((nb, tn), lambda j: (0, j)),
        out_shape=jax.ShapeDtypeStruct((nb, n_out), F32),
        name="adaln",
    )(c, w, b.reshape(1, n_out))


def _inproj_kernel(x_ref, mod_ref, nw_ref, w_ref, b_ref, wg_ref, bg_ref, z_ref, zg_ref):
    h = _rms(x_ref[0], nw_ref[...]) * (1.0 + mod_ref[0, 1:2, :]) + mod_ref[0, 0:1, :]
    z_ref[0] = jnp.dot(h.astype(BF16), w_ref[...], preferred_element_type=F32) + b_ref[...]
    zg_ref[0] = jnp.dot(h, wg_ref[...], preferred_element_type=F32, precision=HIGHEST) + bg_ref[...]


def _inproj(x, mod, nw, w_main, b_main, w_gate, b_gate, tm):
    B, T, _ = x.shape
    return pl.pallas_call(
        _inproj_kernel,
        grid=(B, T // tm),
        in_specs=[pl.BlockSpec((1, tm, D_MODEL), lambda b, t: (b, t, 0)),
                  pl.BlockSpec((1, N_ADA, D_MODEL), lambda b, t: (b, 0, 0)),
                  pl.BlockSpec((1, D_MODEL), lambda b, t: (0, 0)),
                  pl.BlockSpec((D_MODEL, N_MAIN), lambda b, t: (0, 0)),
                  pl.BlockSpec((1, N_MAIN), lambda b, t: (0, 0)),
                  pl.BlockSpec((D_MODEL, LANES), lambda b, t: (0, 0)),
                  pl.BlockSpec((1, LANES), lambda b, t: (0, 0))],
        out_specs=[pl.BlockSpec((1, tm, N_MAIN), lambda b, t: (b, t, 0)),
                   pl.BlockSpec((1, tm, LANES), lambda b, t: (b, t, 0))],
        out_shape=[jax.ShapeDtypeStruct((B, T, N_MAIN), F32),
                   jax.ShapeDtypeStruct((B, T, LANES), F32)],
        compiler_params=pltpu.CompilerParams(
            dimension_semantics=("arbitrary", "arbitrary"), vmem_limit_bytes=48 * 1024 * 1024),
        name="inproj",
    )(x, mod, nw, w_main, b_main, w_gate, b_gate)


def _rglru_kernel(z_ref, h0_ref, conv0_ref, cw_ref, cb_ref, wri_ref, bri_ref, lam_ref,
                  ya_ref, hlast_ref, convnew_ref, xbuf, hcar):
    t = pl.program_id(1)
    tt = z_ref.shape[1]

    @pl.when(t == 0)
    def _():
        xbuf[0:SUBLANES, :] = jnp.zeros((SUBLANES, W_A), F32)
        xbuf[SUBLANES - (CONV_W - 1):SUBLANES, :] = conv0_ref[0]
        hcar[...] = h0_ref[0]

    xbuf[SUBLANES:SUBLANES + tt, :] = z_ref[0, :, 0:W_A]
    u = cb_ref[...]
    for k in range(CONV_W):
        off = SUBLANES - (CONV_W - 1) + k
        u = u + xbuf[off:off + tt, :] * cw_ref[k:k + 1, :]
    convnew_ref[0] = xbuf[tt + SUBLANES - (CONV_W - 1):tt + SUBLANES, :]
    xbuf[0:SUBLANES, :] = xbuf[tt:tt + SUBLANES, :]

    ri = jnp.dot(u.astype(BF16), wri_ref[...], preferred_element_type=F32) + bri_ref[...]
    r = jax.nn.sigmoid(ri[:, 0:W_A])
    i = jax.nn.sigmoid(ri[:, W_A:2 * W_A])
    log_a = (-LRU_C) * r * jax.nn.softplus(-lam_ref[...])
    a = jnp.exp(log_a)
    bx = jnp.sqrt(jnp.tanh(-log_a) * (a * a + 1.0)) * (i * u)

    row = lax.broadcasted_iota(jnp.int32, (tt, W_A), 0)
    d = 1
    while d < tt:
        keep = row >= d
        a_sh = jnp.where(keep, pltpu.roll(a, d, 0), 1.0)
        b_sh = jnp.where(keep, pltpu.roll(bx, d, 0), 0.0)
        bx = bx + a * b_sh
        a = a * a_sh
        d *= 2
    h = bx + a * hcar[...]
    hcar[...] = h[tt - 1:tt, :]
    hlast_ref[0] = h[tt - 1:tt, :]
    ya_ref[0] = (h * jax.nn.gelu(z_ref[0, :, W_A:2 * W_A])).astype(ya_ref.dtype)


def _rglru(z, h0, conv0, conv_w, conv_b, w_ri, b_ri, lam, tt):
    B, T, _ = z.shape
    return pl.pallas_call(
        _rglru_kernel,
        grid=(B, T // tt),
        in_specs=[pl.BlockSpec((1, tt, 2 * W_A), lambda b, t: (b, t, 0)),
                  pl.BlockSpec((1, 1, W_A), lambda b, t: (b, 0, 0)),
                  pl.BlockSpec((1, CONV_W - 1, W_A), lambda b, t: (b, 0, 0)),
                  pl.BlockSpec((CONV_W, W_A), lambda b, t: (0, 0)),
                  pl.BlockSpec((1, W_A), lambda b, t: (0, 0)),
                  pl.BlockSpec((W_A, 2 * W_A), lambda b, t: (0, 0)),
                  pl.BlockSpec((1, 2 * W_A), lambda b, t: (0, 0)),
                  pl.BlockSpec((1, W_A), lambda b, t: (0, 0))],
        out_specs=[pl.BlockSpec((1, tt, W_A), lambda b, t: (b, t, 0)),
                   pl.BlockSpec((1, 1, W_A), lambda b, t: (b, 0, 0)),
                   pl.BlockSpec((1, CONV_W - 1, W_A), lambda b, t: (b, 0, 0))],
        out_shape=[jax.ShapeDtypeStruct((B, T, W_A), BF16),
                   jax.ShapeDtypeStruct((B, 1, W_A), F32),
                   jax.ShapeDtypeStruct((B, CONV_W - 1, W_A), F32)],
        scratch_shapes=[pltpu.VMEM((tt + SUBLANES, W_A), F32), pltpu.VMEM((1, W_A), F32)],
        compiler_params=pltpu.CompilerParams(dimension_semantics=("arbitrary", "arbitrary")),
        name="rglru",
    )(z, h0, conv0, conv_w, conv_b, w_ri, b_ri, lam)


def _mlstm_kernel(L, zqk_ref, zvo_ref, zg_ref, c0_ref, n0_ref, m0_ref, nw_ref,
                  yb_ref, cout_ref, nout_ref, mout_ref, c_s, n_s, m_s):
    t = pl.program_id(1)
    tt = zqk_ref.shape[1]

    @pl.when(t == 0)
    def _():
        c_s[...] = c0_ref[0]
        n_s[...] = n0_ref[0]
        m_s[...] = m0_ref[0]

    ri = lax.broadcasted_iota(jnp.int32, (L, L), 0)
    ci = lax.broadcasted_iota(jnp.int32, (L, L), 1)
    causal = ci <= ri
    tri = causal.astype(F32)
    eye = (ci == ri).astype(F32)

    def to_row(col):
        return jnp.sum(col * eye, axis=0, keepdims=True)

    def chunk(c, carry):
        r0 = pl.multiple_of(c * L, L)
        rows = pl.ds(r0, L)
        g = zg_ref[0, rows, :]
        for hd in range(H_B):
            cols = slice(hd * DK, (hd + 1) * DK)
            q = zqk_ref[0, rows, hd * DK:(hd + 1) * DK] * (DK ** -0.5)
            k = zqk_ref[0, rows, W_B + hd * DK:W_B + (hd + 1) * DK]
            v = zvo_ref[0, rows, hd * DK:(hd + 1) * DK]
            o = zvo_ref[0, rows, W_B + hd * DK:W_B + (hd + 1) * DK]
            ig_c = g[:, hd:hd + 1]
            lf_c = jax.nn.log_sigmoid(g[:, H_B + hd:H_B + hd + 1])
            ig_r = to_row(ig_c)
            b_c = jnp.sum(tri * to_row(lf_c), axis=1, keepdims=True)
            b_r = to_row(b_c)
            m_prev = m_s[hd:hd + 1, 0:1]
            n_prev = n_s[hd:hd + 1, :]
            c_prev = c_s[hd]

            log_d = jnp.where(causal, b_c - b_r + ig_r, -jnp.inf)
            inter = b_c + m_prev
            m_t = jnp.maximum(inter, jnp.max(log_d, axis=1, keepdims=True))
            dmat = jnp.exp(log_d - m_t)
            w_state = jnp.exp(inter - m_t)
            qb = q.astype(BF16)
            s = lax.dot_general(qb, k.astype(BF16), (((1,), (1,)), ((), ())),
                                preferred_element_type=F32) * dmat
            num = (jnp.dot(s.astype(BF16), v.astype(BF16), preferred_element_type=F32)
                   + w_state * jnp.dot(qb, c_prev.astype(BF16), preferred_element_type=F32))
            den = (jnp.sum(s, axis=1, keepdims=True)
                   + w_state * jnp.sum(q * n_prev, axis=1, keepdims=True))
            h = num / jnp.maximum(jnp.abs(den), jnp.exp(-m_t))

            m_new = m_t[L - 1:L, :]
            w_k = jnp.exp(b_c[L - 1:L, :] - b_c + ig_c - m_new)
            w_c = jnp.exp(inter[L - 1:L, :] - m_new)
            kw = k * w_k
            c_s[hd] = w_c * c_prev + jnp.dot(kw.T.astype(BF16), v.astype(BF16),
                                             preferred_element_type=F32)
            n_s[hd:hd + 1, :] = w_c * n_prev + jnp.sum(kw, axis=0, keepdims=True)
            m_s[hd:hd + 1, :] = jnp.broadcast_to(m_new, (1, LANES))

            hn = h * lax.rsqrt(jnp.mean(h * h, axis=1, keepdims=True) + EPS)
            y = jax.nn.sigmoid(o) * (hn * nw_ref[:, cols])
            yb_ref[0, rows, hd * DK:(hd + 1) * DK] = y.astype(yb_ref.dtype)
        return carry

    lax.fori_loop(0, tt // L, chunk, 0)
    cout_ref[0] = c_s[...]
    nout_ref[0] = n_s[...]
    mout_ref[0] = m_s[...]


def _mlstm(z, zg, c0, n0, m0, norm_w, L, tt):
    B, T, _ = z.shape
    state_specs = [pl.BlockSpec((1, H_B, DK, DK), lambda b, t: (b, 0, 0, 0)),
                   pl.BlockSpec((1, H_B, DK), lambda b, t: (b, 0, 0)),
                   pl.BlockSpec((1, H_B, LANES), lambda b, t: (b, 0, 0))]
    return pl.pallas_call(
        functools.partial(_mlstm_kernel, L),
        grid=(B, T // tt),
        in_specs=[pl.BlockSpec((1, tt, 2 * W_B), lambda b, t: (b, t, 1)),
                  pl.BlockSpec((1, tt, 2 * W_B), lambda b, t: (b, t, 2)),
                  pl.BlockSpec((1, tt, LANES), lambda b, t: (b, t, 0))]
        + state_specs + [pl.BlockSpec((1, W_B), lambda b, t: (0, 0))],
        out_specs=[pl.BlockSpec((1, tt, W_B), lambda b, t: (b, t, 0))] + state_specs,
        out_shape=[jax.ShapeDtypeStruct((B, T, W_B), BF16),
                   jax.ShapeDtypeStruct((B, H_B, DK, DK), F32),
                   jax.ShapeDtypeStruct((B, H_B, DK), F32),
                   jax.ShapeDtypeStruct((B, H_B, LANES), F32)],
        scratch_shapes=[pltpu.VMEM((H_B, DK, DK), F32), pltpu.VMEM((H_B, DK), F32),
                        pltpu.VMEM((H_B, LANES), F32)],
        compiler_params=pltpu.CompilerParams(dimension_semantics=("arbitrary", "arbitrary")),
        name="mlstm",
    )(z, z, zg, c0, n0, m0, norm_w)


def _outproj_kernel(x_ref, ya_ref, yb_ref, mod_ref, woa_ref, wob_ref, nw_ref, x1_ref, h2_ref):
    mix = (jnp.dot(ya_ref[0], woa_ref[...], preferred_element_type=F32)
           + jnp.dot(yb_ref[0], wob_ref[...], preferred_element_type=F32))
    x1 = x_ref[0] + mod_ref[0, 2:3, :] * mix
    x1_ref[0] = x1
    h2_ref[0] = _rms(x1, nw_ref[...]) * (1.0 + mod_ref[0, 4:5, :]) + mod_ref[0, 3:4, :]


def _outproj(x, ya, yb, mod, wo_a, wo_b, nw, tm):
    B, T, _ = x.shape
    xspec = pl.BlockSpec((1, tm, D_MODEL), lambda b, t: (b, t, 0))
    yspec = pl.BlockSpec((1, tm, W_A), lambda b, t: (b, t, 0))
    return pl.pallas_call(
        _outproj_kernel,
        grid=(B, T // tm),
        in_specs=[xspec, yspec, yspec,
                  pl.BlockSpec((1, N_ADA, D_MODEL), lambda b, t: (b, 0, 0)),
                  pl.BlockSpec((W_A, D_MODEL), lambda b, t: (0, 0)),
                  pl.BlockSpec((W_B, D_MODEL), lambda b, t: (0, 0)),
                  pl.BlockSpec((1, D_MODEL), lambda b, t: (0, 0))],
        out_specs=[xspec, xspec],
        out_shape=[jax.ShapeDtypeStruct((B, T, D_MODEL), F32)] * 2,
        compiler_params=pltpu.CompilerParams(dimension_semantics=("arbitrary", "arbitrary")),
        name="outproj",
    )(x, ya, yb, mod, wo_a, wo_b, nw)


def _topk_rows(s, k):
    n = s.shape[0]
    iota = lax.broadcasted_iota(jnp.int32, s.shape, 0)
    slot = lax.broadcasted_iota(jnp.int32, (k, s.shape[1]), 0)
    vals = jnp.zeros((k, s.shape[1]), F32)
    idxs = jnp.zeros((k, s.shape[1]), jnp.int32)
    for j in range(k):
        m = jnp.max(s, axis=0, keepdims=True)
        i = jnp.min(jnp.where(s == m, iota, n), axis=0, keepdims=True)
        s = jnp.where(iota == i, -jnp.inf, s)
        vals = jnp.where(slot == j, m, vals)
        idxs = jnp.where(slot == j, i, idxs)
    return vals, idxs


def _select_rows(tbl, sel):
    out = jnp.zeros_like(tbl)
    for i in range(tbl.shape[0]):
        out = jnp.where(sel == i, tbl[i:i + 1, :], out)
    return out


def _retrieve_kernel(h2_ref, wq_ref, k1_ref, k2_ref, e_ref, g_ref):
    q = jnp.dot(h2_ref[...].astype(BF16), wq_ref[...], preferred_element_type=F32)
    half = D_QUERY // 2
    nt = (((1,), (1,)), ((), ()))
    for hd in range(PEER_HEADS):
        qh = q[:, hd * D_QUERY:(hd + 1) * D_QUERY]
        qh = qh * lax.rsqrt(jnp.mean(qh * qh, axis=1, keepdims=True) + EPS)
        s1 = lax.dot_general(k1_ref[...], qh[:, :half], nt, preferred_element_type=F32,
                             precision=HIGHEST)
        s2 = lax.dot_general(k2_ref[...], qh[:, half:], nt, preferred_element_type=F32,
                             precision=HIGHEST)
        v1, i1 = _topk_rows(s1, PEER_TOPK)
        v2, i2 = _topk_rows(s2, PEER_TOPK)
        cand = jnp.concatenate([v1[i:i + 1, :] + v2 for i in range(PEER_TOPK)], axis=0)
        sv, si = _topk_rows(cand, PEER_TOPK)
        e1 = _select_rows(i1, lax.shift_right_logical(si, 4))
        e2 = _select_rows(i2, si & (PEER_TOPK - 1))
        ex = jnp.exp(sv - jnp.max(sv, axis=0, keepdims=True))
        e_ref[hd * PEER_TOPK:(hd + 1) * PEER_TOPK, :] = e1 * N_KEYS + e2
        g_ref[hd * PEER_TOPK:(hd + 1) * PEER_TOPK, :] = ex / jnp.sum(ex, axis=0, keepdims=True)


def _retrieve(h2, wq, k1, k2, tk):
    n = h2.shape[0]
    ospec = pl.BlockSpec((N_PICK, tk), lambda i: (0, i))
    return pl.pallas_call(
        _retrieve_kernel,
        grid=(n // tk,),
        in_specs=[pl.BlockSpec((tk, D_MODEL), lambda i: (i, 0)),
                  pl.BlockSpec((D_MODEL, PEER_HEADS * D_QUERY), lambda i: (0, 0)),
                  pl.BlockSpec((N_KEYS, D_QUERY // 2), lambda i: (0, 0)),
                  pl.BlockSpec((N_KEYS, D_QUERY // 2), lambda i: (0, 0))],
        out_specs=[ospec, ospec],
        out_shape=[jax.ShapeDtypeStruct((N_PICK, n), jnp.int32),
                   jax.ShapeDtypeStruct((N_PICK, n), F32)],
        compiler_params=pltpu.CompilerParams(dimension_semantics=("arbitrary",)),
        name="retrieve",
    )(h2, wq, k1, k2)


def _pack_table(tbl):
    e, d = tbl.shape
    bits = lax.bitcast_convert_type(tbl.astype(BF16), jnp.uint16).astype(jnp.uint32)
    bits = bits.reshape(e // 2, 2, d // LANES, LANES)
    return bits[:, 0] | (bits[:, 1] << 16)


def _expert_row(tbl_ref, e):
    word = tbl_ref[lax.shift_right_logical(e, 1)]
    shift = ((1 - (e & 1)) * 16).astype(jnp.uint32)
    bits = lax.shift_left(word, jnp.full(word.shape, shift, jnp.uint32)) & jnp.uint32(0xFFFF0000)
    return lax.bitcast_convert_type(bits, F32)


def _peer_act_kernel(eid_ref, x_ref, g_ref, tbl_ref, w_ref, r_scr, act_scr):
    tb = x_ref.shape[0]

    def token(p, carry):
        xp = x_ref[p]
        for j in range(N_PICK):
            row = _expert_row(tbl_ref, eid_ref[p, j])
            r_scr[:, j:j + 1] = jnp.sum(row * xp, axis=1, keepdims=True)
        act_scr[pl.ds(p, 1), :] = jnp.sum(r_scr[...], axis=0, keepdims=True)
        return carry

    lax.fori_loop(0, tb, token, 0)
    w_ref[...] = g_ref[...] * jax.nn.gelu(act_scr[...])


def _peer_act(eid, x3, gates, tbl, tb):
    n = eid.shape[0]
    return pl.pallas_call(
        _peer_act_kernel,
        grid=(n // tb,),
        in_specs=[pl.BlockSpec((tb, N_PICK), lambda i: (i, 0), memory_space=pltpu.SMEM),
                  pl.BlockSpec((tb, SUBLANES, LANES), lambda i: (i, 0, 0)),
                  pl.BlockSpec((tb, N_PICK), lambda i: (i, 0)),
                  pl.BlockSpec(memory_space=pltpu.VMEM)],
        out_specs=pl.BlockSpec((tb, N_PICK), lambda i: (i, 0)),
        out_shape=jax.ShapeDtypeStruct((n, N_PICK), F32),
        scratch_shapes=[pltpu.VMEM((SUBLANES, N_PICK), F32), pltpu.VMEM((tb, N_PICK), F32)],
        compiler_params=pltpu.CompilerParams(
            dimension_semantics=("arbitrary",), vmem_limit_bytes=PEER_VMEM_LIMIT),
        name="peer_act",
    )(eid, x3, gates, tbl)


def _peer_out_kernel(eid_ref, w_ref, x1_ref, g2_ref, fw_ref, tbl_ref, y_ref):
    tb = x1_ref.shape[0]
    n_acc = 4

    def token(p, carry):
        accs = [jnp.zeros((SUBLANES, LANES), F32) for _ in range(n_acc)]
        for j in range(N_PICK):
            row = _expert_row(tbl_ref, eid_ref[p, j])
            accs[j % n_acc] = accs[j % n_acc] + w_ref[p, j] * row
        out = (accs[0] + accs[1]) + (accs[2] + accs[3])
        x2 = x1_ref[p] + g2_ref[0] * out
        ms = jnp.sum(jnp.sum(x2 * x2, axis=1, keepdims=True), axis=0, keepdims=True) / D_MODEL
        y_ref[p] = x2 * lax.rsqrt(ms + EPS) * fw_ref[...]
        return carry

    lax.fori_loop(0, tb, token, 0)


def _peer_out(eid, w, x1_3, g2_3, fw_3, tbl, tb, tokens_per_batch):
    n = eid.shape[0]
    sspec = pl.BlockSpec((tb, N_PICK), lambda i: (i, 0), memory_space=pltpu.SMEM)
    xspec = pl.BlockSpec((tb, SUBLANES, LANES), lambda i: (i, 0, 0))
    return pl.pallas_call(
        _peer_out_kernel,
        grid=(n // tb,),
        in_specs=[sspec, sspec, xspec,
                  pl.BlockSpec((1, SUBLANES, LANES), lambda i: ((i * tb) // tokens_per_batch, 0, 0)),
                  pl.BlockSpec((SUBLANES, LANES), lambda i: (0, 0)),
                  pl.BlockSpec(memory_space=pltpu.VMEM)],
        out_specs=xspec,
        out_shape=jax.ShapeDtypeStruct((n, SUBLANES, LANES), F32),
        compiler_params=pltpu.CompilerParams(
            dimension_semantics=("arbitrary",), vmem_limit_bytes=PEER_VMEM_LIMIT),
        name="peer_out",
    )(eid, w, x1_3, g2_3, fw_3, tbl)


def _trunk(x, mod, h0, conv0, c0, n0, m0, chunk_len, p):
    B, T, _ = x.shape
    n = B * T
    tm = min(512, T)
    z, zg = _inproj(x, mod, p["norm1_w"], p["w_main"], p["b_main"], p["w_gate"], p["b_gate"], tm)
    ya, h_last, conv_new = _rglru(z, h0.reshape(B, 1, W_A), conv0, p["conv_w"], p["conv_b"],
                                  p["w_ri"], p["b_ri"], p["lam"], tm)
    m0b = jnp.broadcast_to(m0[:, :, None], (B, H_B, LANES))
    yb, c_new, n_new, m_new = _mlstm(z, zg, c0, n0, m0b, p["mlstm_norm_w"], chunk_len, tm)
    x1, h2 = _outproj(x, ya, yb, mod, p["wo_a"], p["wo_b"], p["norm2_w"], tm)

    tk = 256
    e_t, g_t = _retrieve(h2.reshape(n, D_MODEL), p["wq"], p["k1"], p["k2"], tk)
    eid = e_t.T
    gates = g_t.T
    tb = 32
    w = _peer_act(eid, h2.reshape(n, SUBLANES, LANES), gates, p["u_tbl"], tb)
    y3 = _peer_out(eid, w, x1.reshape(n, SUBLANES, LANES),
                   mod[:, 5].reshape(B, SUBLANES, LANES), p["final_w"], p["v_tbl"], tb, T)
    return (y3.reshape(B, T, D_MODEL), h_last.reshape(B, W_A), conv_new, c_new, n_new,
            m_new[:, :, 0])


def kernel(x_prompt, x_sample, c_prompt, c_sample, state_lru_h, state_lru_conv, state_mlstm_C,
           state_mlstm_n, state_mlstm_m, norm1_w, ada_w, ada_b, w_in, b_in, conv_w, conv_b,
           lru_wr, lru_br, lru_wi, lru_bi, lru_lambda, mlstm_norm_w, w_out, norm2_w,
           peer_wq, peer_k1, peer_k2, peer_u, peer_v, final_norm_w):
    depth = w_in.shape[0]
    assert depth == 1, "single-layer trunk only"
    B = x_prompt.shape[0]
    Bs = x_sample.shape[0]
    l = 0

    def block_diag(w):
        nb, c, d = w.shape
        eye = jnp.eye(nb, dtype=w.dtype)
        return jnp.einsum("ncd,nm->ncmd", w, eye).reshape(nb * c, nb * d)

    pad_g = LANES - N_GATE
    p = {
        "norm1_w": norm1_w[l].reshape(1, D_MODEL),
        "w_main": w_in[l][:, :N_MAIN].astype(BF16),
        "b_main": b_in[l][:N_MAIN].reshape(1, N_MAIN),
        "w_gate": jnp.pad(w_in[l][:, N_MAIN:], ((0, 0), (0, pad_g))),
        "b_gate": jnp.pad(b_in[l][N_MAIN:], (0, pad_g)).reshape(1, LANES),
        "conv_w": conv_w[l],
        "conv_b": conv_b[l].reshape(1, W_A),
        "w_ri": jnp.concatenate([block_diag(lru_wr[l]), block_diag(lru_wi[l])], axis=1).astype(BF16),
        "b_ri": jnp.concatenate([lru_br[l], lru_bi[l]]).reshape(1, 2 * W_A),
        "lam": lru_lambda[l].reshape(1, W_A),
        "mlstm_norm_w": mlstm_norm_w[l].reshape(1, W_B),
        "wo_a": w_out[l][:W_A].astype(BF16),
        "wo_b": w_out[l][W_A:].astype(BF16),
        "norm2_w": norm2_w[l].reshape(1, D_MODEL),
        "wq": peer_wq[l].astype(BF16),
        "k1": peer_k1[l],
        "k2": peer_k2[l],
        "u_tbl": _pack_table(peer_u[l]),
        "v_tbl": _pack_table(peer_v[l]),
        "final_w": final_norm_w.reshape(SUBLANES, LANES),
    }

    mod = _adaln(jnp.concatenate([c_prompt, c_sample], axis=0), ada_w[l], ada_b[l])
    mod = mod.reshape(B + Bs, N_ADA, D_MODEL)

    dt = x_prompt.dtype
    outs_p = _trunk(x_prompt, mod[:B], jnp.zeros((B, W_A), dt), jnp.zeros((B, CONV_W - 1, W_A), dt),
                    jnp.zeros((B, H_B, DK, DK), dt), jnp.zeros((B, H_B, DK), dt),
                    jnp.zeros((B, H_B), dt), CHUNK, p)
    outs_s = _trunk(x_sample, mod[B:], state_lru_h[l], state_lru_conv[l], state_mlstm_C[l],
                    state_mlstm_n[l], state_mlstm_m[l], x_sample.shape[1], p)
    y_p, *st_p = outs_p
    y_s, *st_s = outs_s
    st_p = [s[None].astype(r.dtype) for s, r in zip(
        st_p, (state_lru_h, state_lru_conv, state_mlstm_C, state_mlstm_n, state_mlstm_m))]
    st_s = [s[None].astype(r.dtype) for s, r in zip(
        st_s, (state_lru_h, state_lru_conv, state_mlstm_C, state_mlstm_n, state_mlstm_m))]
    return (y_p, y_s, *st_p, *st_s)
```

```python
import functools

import jax
import jax.numpy as jnp
from jax import lax
from jax.experimental import pallas as pl
from jax.experimental.pallas import tpu as pltpu

D_MODEL = 1024
W_A = 512
W_B = 512
LRU_BLOCKS = 8
CONV_W = 4
LRU_C = 8.0
H_B = 4
DK = W_B // H_B
N_ADA = 6
PEER_HEADS = 8
N_KEYS = 128
D_QUERY = 256
PEER_TOPK = 16
N_PICK = PEER_HEADS * PEER_TOPK
EPS = 1e-6
CHUNK = 64
N_MAIN = 2 * W_A + 4 * W_B
N_GATE = 2 * H_B

LANES = 128
SUBLANES = 8
PEER_VMEM_LIMIT = 48 * 1024 * 1024

F32 = jnp.float32
BF16 = jnp.bfloat16
HIGHEST = lax.Precision.HIGHEST


def _rms(x, w):
    return x * lax.rsqrt(jnp.mean(x * x, axis=-1, keepdims=True) + EPS) * w


def _adaln_kernel(c_ref, w_ref, b_ref, o_ref):
    s = jax.nn.silu(c_ref[...])
    o_ref[...] = jnp.dot(s, w_ref[...], preferred_element_type=F32, precision=HIGHEST) + b_ref[...]


def _adaln(c, w, b):
    nb = c.shape[0]
    n_out = w.shape[1]
    tn = 1024
    return pl.pallas_call(
        _adaln_kernel,
        grid=(n_out // tn,),
        in_specs=[pl.BlockSpec((nb, D_MODEL), lambda j: (0, 0)),
                  pl.BlockSpec((D_MODEL, tn), lambda j: (0, j)),
                  pl.BlockSpec((1, tn), lambda j: (0, j))],
        out_specs=pl.BlockSpec((nb, tn), lambda j: (0, j)),
        out_shape=jax.ShapeDtypeStruct((nb, n_out), F32),
        name="adaln",
    )(c, w, b.reshape(1, n_out))


def _inproj_kernel(x_ref, mod_ref, nw_ref, w_ref, b_ref, wg_ref, bg_ref, z_ref, zg_ref):
    h = _rms(x_ref[0], nw_ref[...]) * (1.0 + mod_ref[0, 1:2, :]) + mod_ref[0, 0:1, :]
    z_ref[0] = jnp.dot(h.astype(BF16), w_ref[...], preferred_element_type=F32) + b_ref[...]
    zg_ref[0] = jnp.dot(h, wg_ref[...], preferred_element_type=F32, precision=HIGHEST) + bg_ref[...]


def _inproj(x, mod, nw, w_main, b_main, w_gate, b_gate, tm):
    B, T, _ = x.shape
    return pl.pallas_call(
        _inproj_kernel,
        grid=(B, T // tm),
        in_specs=[pl.BlockSpec((1, tm, D_MODEL), lambda b, t: (b, t, 0)),
                  pl.BlockSpec((1, N_ADA, D_MODEL), lambda b, t: (b, 0, 0)),
                  pl.BlockSpec((1, D_MODEL), lambda b, t: (0, 0)),
                  pl.BlockSpec((D_MODEL, N_MAIN), lambda b, t: (0, 0)),
                  pl.BlockSpec((1, N_MAIN), lambda b, t: (0, 0)),
                  pl.BlockSpec((D_MODEL, LANES), lambda b, t: (0, 0)),
                  pl.BlockSpec((1, LANES), lambda b, t: (0, 0))],
        out_specs=[pl.BlockSpec((1, tm, N_MAIN), lambda b, t: (b, t, 0)),
                   pl.BlockSpec((1, tm, LANES), lambda b, t: (b, t, 0))],
        out_shape=[jax.ShapeDtypeStruct((B, T, N_MAIN), F32),
                   jax.ShapeDtypeStruct((B, T, LANES), F32)],
        compiler_params=pltpu.CompilerParams(
            dimension_semantics=("arbitrary", "arbitrary"), vmem_limit_bytes=48 * 1024 * 1024),
        name="inproj",
    )(x, mod, nw, w_main, b_main, w_gate, b_gate)


def _rglru_kernel(z_ref, h0_ref, conv0_ref, cw_ref, cb_ref, wri_ref, bri_ref, lam_ref,
                  ya_ref, hlast_ref, convnew_ref, xbuf, hcar):
    t = pl.program_id(1)
    tt = z_ref.shape[1]

    @pl.when(t == 0)
    def _():
        xbuf[0:SUBLANES, :] = jnp.zeros((SUBLANES, W_A), F32)
        xbuf[SUBLANES - (CONV_W - 1):SUBLANES, :] = conv0_ref[0]
        hcar[...] = h0_ref[0]

    xbuf[SUBLANES:SUBLANES + tt, :] = z_ref[0, :, 0:W_A]
    u = cb_ref[...]
    for k in range(CONV_W):
        off = SUBLANES - (CONV_W - 1) + k
        u = u + xbuf[off:off + tt, :] * cw_ref[k:k + 1, :]
    convnew_ref[0] = xbuf[tt + SUBLANES - (CONV_W - 1):tt + SUBLANES, :]
    xbuf[0:SUBLANES, :] = xbuf[tt:tt + SUBLANES, :]

    ri = jnp.dot(u.astype(BF16), wri_ref[...], preferred_element_type=F32) + bri_ref[...]
    r = jax.nn.sigmoid(ri[:, 0:W_A])
    i = jax.nn.sigmoid(ri[:, W_A:2 * W_A])
    log_a = (-LRU_C) * r * jax.nn.softplus(-lam_ref[...])
    a = jnp.exp(log_a)
    bx = jnp.sqrt(jnp.tanh(-log_a) * (a * a + 1.0)) * (i * u)

    row = lax.broadcasted_iota(jnp.int32, (tt, W_A), 0)
    d = 1
    while d < tt:
        keep = row >= d
        a_sh = jnp.where(keep, pltpu.roll(a, d, 0), 1.0)
        b_sh = jnp.where(keep, pltpu.roll(bx, d, 0), 0.0)
        bx = bx + a * b_sh
        a = a * a_sh
        d *= 2
    h = bx + a * hcar[...]
    hcar[...] = h[tt - 1:tt, :]
    hlast_ref[0] = h[tt - 1:tt, :]
    ya_ref[0] = (h * jax.nn.gelu(z_ref[0, :, W_A:2 * W_A])).astype(ya_ref.dtype)


def _rglru(z, h0, conv0, conv_w, conv_b, w_ri, b_ri, lam, tt):
    B, T, _ = z.shape
    return pl.pallas_call(
        _rglru_kernel,
        grid=(B, T // tt),
        in_specs=[pl.BlockSpec((1, tt, 2 * W_A), lambda b, t: (b, t, 0)),
                  pl.BlockSpec((1, 1, W_A), lambda b, t: (b, 0, 0)),
                  pl.BlockSpec((1, CONV_W - 1, W_A), lambda b, t: (b, 0, 0)),
                  pl.BlockSpec((CONV_W, W_A), lambda b, t: (0, 0)),
                  pl.BlockSpec((1, W_A), lambda b, t: (0, 0)),
                  pl.BlockSpec((W_A, 2 * W_A), lambda b, t: (0, 0)),
                  pl.BlockSpec((1, 2 * W_A), lambda b, t: (0, 0)),
                  pl.BlockSpec((1, W_A), lambda b, t: (0, 0))],
        out_specs=[pl.BlockSpec((1, tt, W_A), lambda b, t: (b, t, 0)),
                   pl.BlockSpec((1, 1, W_A), lambda b, t: (b, 0, 0)),
                   pl.BlockSpec((1, CONV_W - 1, W_A), lambda b, t: (b, 0, 0))],
        out_shape=[jax.ShapeDtypeStruct((B, T, W_A), BF16),
                   jax.ShapeDtypeStruct((B, 1, W_A), F32),
                   jax.ShapeDtypeStruct((B, CONV_W - 1, W_A), F32)],
        scratch_shapes=[pltpu.VMEM((tt + SUBLANES, W_A), F32), pltpu.VMEM((1, W_A), F32)],
        compiler_params=pltpu.CompilerParams(dimension_semantics=("arbitrary", "arbitrary")),
        name="rglru",
    )(z, h0, conv0, conv_w, conv_b, w_ri, b_ri, lam)


def _mlstm_kernel(L, zqk_ref, zvo_ref, zg_ref, c0_ref, n0_ref, m0_ref, nw_ref,
                  yb_ref, cout_ref, nout_ref, mout_ref, c_s, n_s, m_s):
    t = pl.program_id(1)
    tt = zqk_ref.shape[1]

    @pl.when(t == 0)
    def _():
        c_s[...] = c0_ref[0]
        n_s[...] = n0_ref[0]
        m_s[...] = m0_ref[0]

    ri = lax.broadcasted_iota(jnp.int32, (L, L), 0)
    ci = lax.broadcasted_iota(jnp.int32, (L, L), 1)
    causal = ci <= ri
    tri = causal.astype(F32)
    eye = (ci == ri).astype(F32)

    def to_row(col):
        return jnp.sum(col * eye, axis=0, keepdims=True)

    def chunk(c, carry):
        r0 = pl.multiple_of(c * L, L)
        rows = pl.ds(r0, L)
        g = zg_ref[0, rows, :]
        for hd in range(H_B):
            cols = slice(hd * DK, (hd + 1) * DK)
            q = zqk_ref[0, rows, hd * DK:(hd + 1) * DK] * (DK ** -0.5)
            k = zqk_ref[0, rows, W_B + hd * DK:W_B + (hd + 1) * DK]
            v = zvo_ref[0, rows, hd * DK:(hd + 1) * DK]
            o = zvo_ref[0, rows, W_B + hd * DK:W_B + (hd + 1) * DK]
            ig_c = g[:, hd:hd + 1]
            lf_c = jax.nn.log_sigmoid(g[:, H_B + hd:H_B + hd + 1])
            ig_r = to_row(ig_c)
            b_c = jnp.sum(tri * to_row(lf_c), axis=1, keepdims=True)
            b_r = to_row(b_c)
            m_prev = m_s[hd:hd + 1, 0:1]
            n_prev = n_s[hd:hd + 1, :]
            c_prev = c_s[hd]

            log_d = jnp.where(causal, b_c - b_r + ig_r, -jnp.inf)
            inter = b_c + m_prev
            m_t = jnp.maximum(inter, jnp.max(log_d, axis=1, keepdims=True))
            dmat = jnp.exp(log_d - m_t)
            w_state = jnp.exp(inter - m_t)
            qb = q.astype(BF16)
            s = lax.dot_general(qb, k.astype(BF16), (((1,), (1,)), ((), ())),
                                preferred_element_type=F32) * dmat
            num = (jnp.dot(s.astype(BF16), v.astype(BF16), preferred_element_type=F32)
                   + w_state * jnp.dot(qb, c_prev.astype(BF16), preferred_element_type=F32))
            den = (jnp.sum(s, axis=1, keepdims=True)
                   + w_state * jnp.sum(q * n_prev, axis=1, keepdims=True))
            h = num / jnp.maximum(jnp.abs(den), jnp.exp(-m_t))

            m_new = m_t[L - 1:L, :]
            w_k = jnp.exp(b_c[L - 1:L, :] - b_c + ig_c - m_new)
            w_c = jnp.exp(inter[L - 1:L, :] - m_new)
            kw = k * w_k
            c_s[hd] = w_c * c_prev + jnp.dot(kw.T.astype(BF16), v.astype(BF16),
                                             preferred_element_type=F32)
            n_s[hd:hd + 1, :] = w_c * n_prev + jnp.sum(kw, axis=0, keepdims=True)
            m_s[hd:hd + 1, :] = jnp.broadcast_to(m_new, (1, LANES))

            hn = h * lax.rsqrt(jnp.mean(h * h, axis=1, keepdims=True) + EPS)
            y = jax.nn.sigmoid(o) * (hn * nw_ref[:, cols])
            yb_ref[0, rows, hd * DK:(hd + 1) * DK] = y.astype(yb_ref.dtype)
        return carry

    lax.fori_loop(0, tt // L, chunk, 0)
    cout_ref[0] = c_s[...]
    nout_ref[0] = n_s[...]
    mout_ref[0] = m_s[...]


def _mlstm(z, zg, c0, n0, m0, norm_w, L, tt):
    B, T, _ = z.shape
    state_specs = [pl.BlockSpec((1, H_B, DK, DK), lambda b, t: (b, 0, 0, 0)),
                   pl.BlockSpec((1, H_B, DK), lambda b, t: (b, 0, 0)),
                   pl.BlockSpec((1, H_B, LANES), lambda b, t: (b, 0, 0))]
    return pl.pallas_call(
        functools.partial(_mlstm_kernel, L),
        grid=(B, T // tt),
        in_specs=[pl.BlockSpec((1, tt, 2 * W_B), lambda b, t: (b, t, 1)),
                  pl.BlockSpec((1, tt, 2 * W_B), lambda b, t: (b, t, 2)),
                  pl.BlockSpec((1, tt, LANES), lambda b, t: (b, t, 0))]
        + state_specs + [pl.BlockSpec((1, W_B), lambda b, t: (0, 0))],
        out_specs=[pl.BlockSpec((1, tt, W_B), lambda b, t: (b, t, 0))] + state_specs,
        out_shape=[jax.ShapeDtypeStruct((B, T, W_B), BF16),
                   jax.ShapeDtypeStruct((B, H_B, DK, DK), F32),
                   jax.ShapeDtypeStruct((B, H_B, DK), F32),
                   jax.ShapeDtypeStruct((B, H_B, LANES), F32)],
        scratch_shapes=[pltpu.VMEM((H_B, DK, DK), F32), pltpu.VMEM((H_B, DK), F32),
                        pltpu.VMEM((H_B, LANES), F32)],
        compiler_params=pltpu.CompilerParams(dimension_semantics=("arbitrary", "arbitrary")),
        name="mlstm",
    )(z, z, zg, c0, n0, m0, norm_w)


def _outproj_kernel(x_ref, ya_ref, yb_ref, mod_ref, woa_ref, wob_ref, nw_ref, x1_ref, h2_ref):
    mix = (jnp.dot(ya_ref[0], woa_ref[...], preferred_element_type=F32)
           + jnp.dot(yb_ref[0], wob_ref[...], preferred_element_type=F32))
    x1 = x_ref[0] + mod_ref[0, 2:3, :] * mix
    x1_ref[0] = x1
    h2_ref[0] = _rms(x1, nw_ref[...]) * (1.0 + mod_ref[0, 4:5, :]) + mod_ref[0, 3:4, :]


def _outproj(x, ya, yb, mod, wo_a, wo_b, nw, tm):
    B, T, _ = x.shape
    xspec = pl.BlockSpec((1, tm, D_MODEL), lambda b, t: (b, t, 0))
    yspec = pl.BlockSpec((1, tm, W_A), lambda b, t: (b, t, 0))
    return pl.pallas_call(
        _outproj_kernel,
        grid=(B, T // tm),
        in_specs=[xspec, yspec, yspec,
                  pl.BlockSpec((1, N_ADA, D_MODEL), lambda b, t: (b, 0, 0)),
                  pl.BlockSpec((W_A, D_MODEL), lambda b, t: (0, 0)),
                  pl.BlockSpec((W_B, D_MODEL), lambda b, t: (0, 0)),
                  pl.BlockSpec((1, D_MODEL), lambda b, t: (0, 0))],
        out_specs=[xspec, xspec],
        out_shape=[jax.ShapeDtypeStruct((B, T, D_MODEL), F32)] * 2,
        compiler_params=pltpu.CompilerParams(dimension_semantics=("arbitrary", "arbitrary")),
        name="outproj",
    )(x, ya, yb, mod, wo_a, wo_b, nw)


def _topk_rows(s, ids, k):
    big = jnp.int32(2 ** 30)
    slot = lax.broadcasted_iota(jnp.int32, (k, s.shape[1]), 0)
    vals = jnp.zeros((k, s.shape[1]), F32)
    idxs = jnp.zeros((k, s.shape[1]), jnp.int32)
    for j in range(k):
        m = jnp.max(s, axis=0, keepdims=True)
        i = jnp.min(jnp.where(s == m, ids, big), axis=0, keepdims=True)
        s = jnp.where(ids == i, -jnp.inf, s)
        vals = jnp.where(slot == j, m, vals)
        idxs = jnp.where(slot == j, i, idxs)
    return vals, idxs


def _pair_candidates(v1, v2):
    k, tk = v1.shape
    row = lax.broadcasted_iota(jnp.int32, (SUBLANES, tk), 0)
    sums = [v1[0:1, :] + v2]
    ids = [lax.broadcasted_iota(jnp.int32, (k, tk), 0)]
    for i in range(1, SUBLANES):
        blk = v1[i:i + 1, :] + v2[0:SUBLANES, :]
        sums.append(jnp.where(row < k // (i + 1), blk, -jnp.inf))
        ids.append(row + i * k)
    for i0 in range(SUBLANES, k, SUBLANES):
        sums.append(v1[i0:i0 + SUBLANES, :] + v2[0:1, :])
        ids.append((row + i0) * k)
    return jnp.concatenate(sums, axis=0), jnp.concatenate(ids, axis=0)


def _select_rows(tbl, sel):
    out = jnp.zeros_like(tbl)
    for i in range(tbl.shape[0]):
        out = jnp.where(sel == i, tbl[i:i + 1, :], out)
    return out


def _retrieve_kernel(h2_ref, wq_ref, k1_ref, k2_ref, tix_ref, par_ref, g_ref):
    q = jnp.dot(h2_ref[...].astype(BF16), wq_ref[...], preferred_element_type=F32)
    half = D_QUERY // 2
    nt = (((1,), (1,)), ((), ()))
    key_ids = lax.broadcasted_iota(jnp.int32, (N_KEYS, q.shape[0]), 0)
    for hd in range(PEER_HEADS):
        qh = q[:, hd * D_QUERY:(hd + 1) * D_QUERY]
        qh = qh * lax.rsqrt(jnp.mean(qh * qh, axis=1, keepdims=True) + EPS)
        s1 = lax.dot_general(k1_ref[...], qh[:, :half], nt, preferred_element_type=F32,
                             precision=HIGHEST)
        s2 = lax.dot_general(k2_ref[...], qh[:, half:], nt, preferred_element_type=F32,
                             precision=HIGHEST)
        v1, i1 = _topk_rows(s1, key_ids, PEER_TOPK)
        v2, i2 = _topk_rows(s2, key_ids, PEER_TOPK)
        cand, cand_ids = _pair_candidates(v1, v2)
        sv, si = _topk_rows(cand, cand_ids, PEER_TOPK)
        e1 = _select_rows(i1, lax.shift_right_logical(si, 4))
        e2 = _select_rows(i2, si & (PEER_TOPK - 1))
        expert = e1 * N_KEYS + e2
        ex = jnp.exp(sv - jnp.max(sv, axis=0, keepdims=True))
        rows = slice(hd * PEER_TOPK, (hd + 1) * PEER_TOPK)
        tix_ref[rows, :] = lax.shift_right_logical(expert, 1)
        par_ref[rows, :] = expert & 1
        g_ref[rows, :] = ex / jnp.sum(ex, axis=0, keepdims=True)


def _retrieve(h2, wq, k1, k2, tk):
    n = h2.shape[0]
    ospec = pl.BlockSpec((N_PICK, tk), lambda i: (0, i))
    return pl.pallas_call(
        _retrieve_kernel,
        grid=(n // tk,),
        in_specs=[pl.BlockSpec((tk, D_MODEL), lambda i: (i, 0)),
                  pl.BlockSpec((D_MODEL, PEER_HEADS * D_QUERY), lambda i: (0, 0)),
                  pl.BlockSpec((N_KEYS, D_QUERY // 2), lambda i: (0, 0)),
                  pl.BlockSpec((N_KEYS, D_QUERY // 2), lambda i: (0, 0))],
        out_specs=[ospec, ospec, ospec],
        out_shape=[jax.ShapeDtypeStruct((N_PICK, n), jnp.int32),
                   jax.ShapeDtypeStruct((N_PICK, n), jnp.int32),
                   jax.ShapeDtypeStruct((N_PICK, n), F32)],
        compiler_params=pltpu.CompilerParams(dimension_semantics=("arbitrary",)),
        name="retrieve",
    )(h2, wq, k1, k2)


HALF = SUBLANES // 2


def _pack_table(tbl):
    e, d = tbl.shape
    bits = lax.bitcast_convert_type(tbl.astype(BF16), jnp.uint16).astype(jnp.uint32)
    bits = bits.reshape(e // 2, 2, 2, HALF, LANES)
    words = bits[:, :, 0] | (bits[:, :, 1] << 16)
    return words.reshape(e // 2, SUBLANES, LANES)


def _expert_tile(tbl_ref, t):
    word = tbl_ref[t]
    lo = lax.bitcast_convert_type(word << 16, F32)
    hi = lax.bitcast_convert_type(word & jnp.uint32(0xFFFF0000), F32)
    return lo, hi


def _peer_act_kernel(tix_ref, x_ref, par_ref, g_ref, tbl_ref, w_ref, r_scr, act_scr):
    tb = x_ref.shape[0]

    def token(p, carry):
        xp = x_ref[p]
        xl = jnp.concatenate([xp[0:HALF], xp[0:HALF]], axis=0)
        xh = jnp.concatenate([xp[HALF:], xp[HALF:]], axis=0)
        for j in range(N_PICK):
            lo, hi = _expert_tile(tbl_ref, tix_ref[p, j])
            r_scr[:, j:j + 1] = jnp.sum(lo * xl + hi * xh, axis=1, keepdims=True)
        r = r_scr[...]
        even = jnp.sum(r[0:HALF], axis=0, keepdims=True)
        odd = jnp.sum(r[HALF:], axis=0, keepdims=True)
        act_scr[pl.ds(p, 1), :] = jnp.where(par_ref[pl.ds(p, 1), :] == 1, odd, even)
        return carry

    lax.fori_loop(0, tb, token, 0)
    w_ref[...] = g_ref[...] * jax.nn.gelu(act_scr[...])


def _peer_act(tix, x3, par, gates, tbl, tb):
    n = tix.shape[0]
    vspec = pl.BlockSpec((tb, N_PICK), lambda i: (i, 0))
    return pl.pallas_call(
        _peer_act_kernel,
        grid=(n // tb,),
        in_specs=[pl.BlockSpec((tb, N_PICK), lambda i: (i, 0), memory_space=pltpu.SMEM),
                  pl.BlockSpec((tb, SUBLANES, LANES), lambda i: (i, 0, 0)),
                  vspec, vspec,
                  pl.BlockSpec(memory_space=pltpu.VMEM)],
        out_specs=vspec,
        out_shape=jax.ShapeDtypeStruct((n, N_PICK), F32),
        scratch_shapes=[pltpu.VMEM((SUBLANES, N_PICK), F32), pltpu.VMEM((tb, N_PICK), F32)],
        compiler_params=pltpu.CompilerParams(
            dimension_semantics=("arbitrary",), vmem_limit_bytes=PEER_VMEM_LIMIT),
        name="peer_act",
    )(tix, x3, par, gates, tbl)


TILE_ROWS = 2 * SUBLANES


def _peer_out_kernel(tix_ref, w_ref, par_ref, x1_ref, g2_ref, fw_ref, tbl_ref, y_ref,
                     wexp_scr, rows_a, rows_b):
    tb = x1_ref.shape[0]
    kg = N_PICK * TILE_ROWS

    col_pick = lax.shift_right_logical(lax.broadcasted_iota(jnp.int32, (N_PICK, kg), 1), 4)
    expand = (col_pick == lax.broadcasted_iota(jnp.int32, (N_PICK, kg), 0)).astype(BF16)
    w = w_ref[...]
    w_hi = w.astype(BF16)
    w_lo = (w - w_hi.astype(F32)).astype(BF16)
    par8 = (par_ref[...] * SUBLANES).astype(F32).astype(BF16)
    wexp_scr[0:tb, :] = jnp.dot(w_hi, expand, preferred_element_type=F32)
    wexp_scr[tb:2 * tb, :] = jnp.dot(w_lo, expand, preferred_element_type=F32)
    wexp_scr[2 * tb:3 * tb, :] = jnp.dot(par8, expand, preferred_element_type=F32)

    rel = ((lax.broadcasted_iota(jnp.int32, (SUBLANES, kg), 1) & (TILE_ROWS - 1))
           - lax.broadcasted_iota(jnp.int32, (SUBLANES, kg), 0)).astype(F32)

    def gather(p, rows_scr):
        for j in range(N_PICK):
            rows_scr[j] = tbl_ref[tix_ref[p, j]]

    def reduce(p, rows_scr):
        sel = rel == wexp_scr[pl.ds(2 * tb + p, 1), :]
        a_hi = jnp.where(sel, wexp_scr[pl.ds(p, 1), :], 0.0)
        a_lo = jnp.where(sel, wexp_scr[pl.ds(tb + p, 1), :], 0.0)
        a = jnp.concatenate([a_hi, a_lo], axis=0).astype(BF16)
        res = jnp.dot(a, rows_scr[...].reshape(kg, LANES), preferred_element_type=F32)
        y_ref[p] = res[0:SUBLANES] + res[SUBLANES:]

    gather(0, rows_a)

    def pair(i, carry):
        p = 2 * i
        gather(p + 1, rows_b)
        reduce(p, rows_a)
        gather(jnp.minimum(p + 2, tb - 1), rows_a)
        reduce(p + 1, rows_b)
        return carry

    lax.fori_loop(0, tb // 2, pair, 0)

    x2 = x1_ref[...] + g2_ref[...] * y_ref[...]
    ms = jnp.sum(jnp.sum(x2 * x2, axis=2, keepdims=True), axis=1, keepdims=True) / D_MODEL
    y_ref[...] = x2 * lax.rsqrt(ms + EPS) * fw_ref[...]


def _peer_out(tix, w, par, x1_3, g2_3, fw_3, tbl, tb, tokens_per_batch):
    n = tix.shape[0]
    vspec = pl.BlockSpec((tb, N_PICK), lambda i: (i, 0))
    xspec = pl.BlockSpec((tb, SUBLANES, LANES), lambda i: (i, 0, 0))
    return pl.pallas_call(
        _peer_out_kernel,
        grid=(n // tb,),
        in_specs=[pl.BlockSpec((tb, N_PICK), lambda i: (i, 0), memory_space=pltpu.SMEM),
                  vspec, vspec, xspec,
                  pl.BlockSpec((1, SUBLANES, LANES), lambda i: ((i * tb) // tokens_per_batch, 0, 0)),
                  pl.BlockSpec((SUBLANES, LANES), lambda i: (0, 0)),
                  pl.BlockSpec(memory_space=pltpu.VMEM)],
        out_specs=xspec,
        out_shape=jax.ShapeDtypeStruct((n, SUBLANES, LANES), F32),
        scratch_shapes=[pltpu.VMEM((3 * tb, N_PICK * TILE_ROWS), F32),
                        pltpu.VMEM((N_PICK, TILE_ROWS, LANES), BF16),
                        pltpu.VMEM((N_PICK, TILE_ROWS, LANES), BF16)],
        compiler_params=pltpu.CompilerParams(
            dimension_semantics=("arbitrary",), vmem_limit_bytes=PEER_VMEM_LIMIT),
        name="peer_out",
    )(tix, w, par, x1_3, g2_3, fw_3, tbl)


def _trunk(x, mod, h0, conv0, c0, n0, m0, chunk_len, p):
    B, T, _ = x.shape
    n = B * T
    tm = min(512, T)
    z, zg = _inproj(x, mod, p["norm1_w"], p["w_main"], p["b_main"], p["w_gate"], p["b_gate"], tm)
    ya, h_last, conv_new = _rglru(z, h0.reshape(B, 1, W_A), conv0, p["conv_w"], p["conv_b"],
                                  p["w_ri"], p["b_ri"], p["lam"], tm)
    m0b = jnp.broadcast_to(m0[:, :, None], (B, H_B, LANES))
    yb, c_new, n_new, m_new = _mlstm(z, zg, c0, n0, m0b, p["mlstm_norm_w"], chunk_len, tm)
    x1, h2 = _outproj(x, ya, yb, mod, p["wo_a"], p["wo_b"], p["norm2_w"], tm)

    tk = 256
    tix_t, par_t, g_t = _retrieve(h2.reshape(n, D_MODEL), p["wq"], p["k1"], p["k2"], tk)
    tix = tix_t.T
    par = par_t.T
    tb = 32
    w = _peer_act(tix, h2.reshape(n, SUBLANES, LANES), par, g_t.T, p["u_tbl"], tb)
    y3 = _peer_out(tix, w, par, x1.reshape(n, SUBLANES, LANES),
                   mod[:, 5].reshape(B, SUBLANES, LANES), p["final_w"], p["v_tbl"], tb, T)
    return (y3.reshape(B, T, D_MODEL), h_last.reshape(B, W_A), conv_new, c_new, n_new,
            m_new[:, :, 0])


def kernel(x_prompt, x_sample, c_prompt, c_sample, state_lru_h, state_lru_conv, state_mlstm_C,
           state_mlstm_n, state_mlstm_m, norm1_w, ada_w, ada_b, w_in, b_in, conv_w, conv_b,
           lru_wr, lru_br, lru_wi, lru_bi, lru_lambda, mlstm_norm_w, w_out, norm2_w,
           peer_wq, peer_k1, peer_k2, peer_u, peer_v, final_norm_w):
    depth = w_in.shape[0]
    assert depth == 1, "single-layer trunk only"
    B = x_prompt.shape[0]
    Bs = x_sample.shape[0]
    l = 0

    def block_diag(w):
        nb, c, d = w.shape
        eye = jnp.eye(nb, dtype=w.dtype)
        return jnp.einsum("ncd,nm->ncmd", w, eye).reshape(nb * c, nb * d)

    pad_g = LANES - N_GATE
    p = {
        "norm1_w": norm1_w[l].reshape(1, D_MODEL),
        "w_main": w_in[l][:, :N_MAIN].astype(BF16),
        "b_main": b_in[l][:N_MAIN].reshape(1, N_MAIN),
        "w_gate": jnp.pad(w_in[l][:, N_MAIN:], ((0, 0), (0, pad_g))),
        "b_gate": jnp.pad(b_in[l][N_MAIN:], (0, pad_g)).reshape(1, LANES),
        "conv_w": conv_w[l],
        "conv_b": conv_b[l].reshape(1, W_A),
        "w_ri": jnp.concatenate([block_diag(lru_wr[l]), block_diag(lru_wi[l])], axis=1).astype(BF16),
        "b_ri": jnp.concatenate([lru_br[l], lru_bi[l]]).reshape(1, 2 * W_A),
        "lam": lru_lambda[l].reshape(1, W_A),
        "mlstm_norm_w": mlstm_norm_w[l].reshape(1, W_B),
        "wo_a": w_out[l][:W_A].astype(BF16),
        "wo_b": w_out[l][W_A:].astype(BF16),
        "norm2_w": norm2_w[l].reshape(1, D_MODEL),
        "wq": peer_wq[l].astype(BF16),
        "k1": peer_k1[l],
        "k2": peer_k2[l],
        "u_tbl": _pack_table(peer_u[l]),
        "v_tbl": peer_v[l].astype(BF16).reshape(-1, TILE_ROWS, LANES),
        "final_w": final_norm_w.reshape(SUBLANES, LANES),
    }

    mod = _adaln(jnp.concatenate([c_prompt, c_sample], axis=0), ada_w[l], ada_b[l])
    mod = mod.reshape(B + Bs, N_ADA, D_MODEL)

    dt = x_prompt.dtype
    outs_p = _trunk(x_prompt, mod[:B], jnp.zeros((B, W_A), dt), jnp.zeros((B, CONV_W - 1, W_A), dt),
                    jnp.zeros((B, H_B, DK, DK), dt), jnp.zeros((B, H_B, DK), dt),
                    jnp.zeros((B, H_B), dt), CHUNK, p)
    outs_s = _trunk(x_sample, mod[B:], state_lru_h[l], state_lru_conv[l], state_mlstm_C[l],
                    state_mlstm_n[l], state_mlstm_m[l], x_sample.shape[1], p)
    y_p, *st_p = outs_p
    y_s, *st_s = outs_s
    st_p = [s[None].astype(r.dtype) for s, r in zip(
        st_p, (state_lru_h, state_lru_conv, state_mlstm_C, state_mlstm_n, state_mlstm_m))]
    st_s = [s[None].astype(r.dtype) for s, r in zip(
        st_s, (state_lru_h, state_lru_conv, state_mlstm_C, state_mlstm_n, state_mlstm_m))]
    return (y_p, y_s, *st_p, *st_s)
```

```python
import functools

import jax
import jax.numpy as jnp
from jax import lax
from jax.experimental import pallas as pl
from jax.experimental.pallas import tpu as pltpu

D_MODEL = 1024
W_A = 512
W_B = 512
LRU_BLOCKS = 8
CONV_W = 4
LRU_C = 8.0
H_B = 4
DK = W_B // H_B
N_ADA = 6
PEER_HEADS = 8
N_KEYS = 128
D_QUERY = 256
PEER_TOPK = 16
N_PICK = PEER_HEADS * PEER_TOPK
EPS = 1e-6
CHUNK = 64
N_MAIN = 2 * W_A + 4 * W_B
N_GATE = 2 * H_B

LANES = 128
SUBLANES = 8
PEER_VMEM_LIMIT = 48 * 1024 * 1024

F32 = jnp.float32
BF16 = jnp.bfloat16
HIGHEST = lax.Precision.HIGHEST


def _rms(x, w):
    return x * lax.rsqrt(jnp.mean(x * x, axis=-1, keepdims=True) + EPS) * w


def _adaln_kernel(c_ref, w_ref, b_ref, o_ref):
    s = jax.nn.silu(c_ref[...])
    o_ref[...] = jnp.dot(s, w_ref[...], preferred_element_type=F32, precision=HIGHEST) + b_ref[...]


def _adaln(c, w, b):
    nb = c.shape[0]
    n_out = w.shape[1]
    tn = 1024
    return pl.pallas_call(
        _adaln_kernel,
        grid=(n_out // tn,),
        in_specs=[pl.BlockSpec((nb, D_MODEL), lambda j: (0, 0)),
                  pl.BlockSpec((D_MODEL, tn), lambda j: (0, j)),
                  pl.BlockSpec((1, tn), lambda j: (0, j))],
        out_specs=pl.BlockSpec((nb, tn), lambda j: (0, j)),
        out_shape=jax.ShapeDtypeStruct((nb, n_out), F32),
        name="adaln",
    )(c, w, b.reshape(1, n_out))


def _inproj_kernel(x_ref, mod_ref, nw_ref, w_ref, b_ref, wg_ref, bg_ref, z_ref, zg_ref):
    h = _rms(x_ref[0], nw_ref[...]) * (1.0 + mod_ref[0, 1:2, :]) + mod_ref[0, 0:1, :]
    z_ref[0] = jnp.dot(h.astype(BF16), w_ref[...], preferred_element_type=F32) + b_ref[...]
    zg_ref[0] = jnp.dot(h, wg_ref[...], preferred_element_type=F32, precision=HIGHEST) + bg_ref[...]


def _inproj(x, mod, nw, w_main, b_main, w_gate, b_gate, tm):
    B, T, _ = x.shape
    return pl.pallas_call(
        _inproj_kernel,
        grid=(B, T // tm),
        in_specs=[pl.BlockSpec((1, tm, D_MODEL), lambda b, t: (b, t, 0)),
                  pl.BlockSpec((1, N_ADA, D_MODEL), lambda b, t: (b, 0, 0)),
                  pl.BlockSpec((1, D_MODEL), lambda b, t: (0, 0)),
                  pl.BlockSpec((D_MODEL, N_MAIN), lambda b, t: (0, 0)),
                  pl.BlockSpec((1, N_MAIN), lambda b, t: (0, 0)),
                  pl.BlockSpec((D_MODEL, LANES), lambda b, t: (0, 0)),
                  pl.BlockSpec((1, LANES), lambda b, t: (0, 0))],
        out_specs=[pl.BlockSpec((1, tm, N_MAIN), lambda b, t: (b, t, 0)),
                   pl.BlockSpec((1, tm, LANES), lambda b, t: (b, t, 0))],
        out_shape=[jax.ShapeDtypeStruct((B, T, N_MAIN), F32),
                   jax.ShapeDtypeStruct((B, T, LANES), F32)],
        compiler_params=pltpu.CompilerParams(
            dimension_semantics=("arbitrary", "arbitrary"), vmem_limit_bytes=48 * 1024 * 1024),
        name="inproj",
    )(x, mod, nw, w_main, b_main, w_gate, b_gate)


def _rglru_kernel(z_ref, h0_ref, conv0_ref, cw_ref, cb_ref, wri_ref, bri_ref, lam_ref,
                  ya_ref, hlast_ref, convnew_ref, xbuf, hcar):
    t = pl.program_id(1)
    tt = z_ref.shape[1]

    @pl.when(t == 0)
    def _():
        xbuf[0:SUBLANES, :] = jnp.zeros((SUBLANES, W_A), F32)
        xbuf[SUBLANES - (CONV_W - 1):SUBLANES, :] = conv0_ref[0]
        hcar[...] = h0_ref[0]

    xbuf[SUBLANES:SUBLANES + tt, :] = z_ref[0, :, 0:W_A]
    u = cb_ref[...]
    for k in range(CONV_W):
        off = SUBLANES - (CONV_W - 1) + k
        u = u + xbuf[off:off + tt, :] * cw_ref[k:k + 1, :]
    convnew_ref[0] = xbuf[tt + SUBLANES - (CONV_W - 1):tt + SUBLANES, :]
    xbuf[0:SUBLANES, :] = xbuf[tt:tt + SUBLANES, :]

    ri = jnp.dot(u.astype(BF16), wri_ref[...], preferred_element_type=F32) + bri_ref[...]
    r = jax.nn.sigmoid(ri[:, 0:W_A])
    i = jax.nn.sigmoid(ri[:, W_A:2 * W_A])
    log_a = (-LRU_C) * r * jax.nn.softplus(-lam_ref[...])
    a = jnp.exp(log_a)
    bx = jnp.sqrt(jnp.tanh(-log_a) * (a * a + 1.0)) * (i * u)

    row = lax.broadcasted_iota(jnp.int32, (tt, W_A), 0)
    d = 1
    while d < tt:
        keep = row >= d
        a_sh = jnp.where(keep, pltpu.roll(a, d, 0), 1.0)
        b_sh = jnp.where(keep, pltpu.roll(bx, d, 0), 0.0)
        bx = bx + a * b_sh
        a = a * a_sh
        d *= 2
    h = bx + a * hcar[...]
    hcar[...] = h[tt - 1:tt, :]
    hlast_ref[0] = h[tt - 1:tt, :]
    ya_ref[0] = (h * jax.nn.gelu(z_ref[0, :, W_A:2 * W_A])).astype(ya_ref.dtype)


def _rglru(z, h0, conv0, conv_w, conv_b, w_ri, b_ri, lam, tt):
    B, T, _ = z.shape
    return pl.pallas_call(
        _rglru_kernel,
        grid=(B, T // tt),
        in_specs=[pl.BlockSpec((1, tt, 2 * W_A), lambda b, t: (b, t, 0)),
                  pl.BlockSpec((1, 1, W_A), lambda b, t: (b, 0, 0)),
                  pl.BlockSpec((1, CONV_W - 1, W_A), lambda b, t: (b, 0, 0)),
                  pl.BlockSpec((CONV_W, W_A), lambda b, t: (0, 0)),
                  pl.BlockSpec((1, W_A), lambda b, t: (0, 0)),
                  pl.BlockSpec((W_A, 2 * W_A), lambda b, t: (0, 0)),
                  pl.BlockSpec((1, 2 * W_A), lambda b, t: (0, 0)),
                  pl.BlockSpec((1, W_A), lambda b, t: (0, 0))],
        out_specs=[pl.BlockSpec((1, tt, W_A), lambda b, t: (b, t, 0)),
                   pl.BlockSpec((1, 1, W_A), lambda b, t: (b, 0, 0)),
                   pl.BlockSpec((1, CONV_W - 1, W_A), lambda b, t: (b, 0, 0))],
        out_shape=[jax.ShapeDtypeStruct((B, T, W_A), BF16),
                   jax.ShapeDtypeStruct((B, 1, W_A), F32),
                   jax.ShapeDtypeStruct((B, CONV_W - 1, W_A), F32)],
        scratch_shapes=[pltpu.VMEM((tt + SUBLANES, W_A), F32), pltpu.VMEM((1, W_A), F32)],
        compiler_params=pltpu.CompilerParams(dimension_semantics=("arbitrary", "arbitrary")),
        name="rglru",
    )(z, h0, conv0, conv_w, conv_b, w_ri, b_ri, lam)


def _mlstm_kernel(L, zqk_ref, zvo_ref, zg_ref, c0_ref, n0_ref, m0_ref, nw_ref,
                  yb_ref, cout_ref, nout_ref, mout_ref, c_s, n_s, m_s):
    t = pl.program_id(1)
    tt = zqk_ref.shape[1]

    @pl.when(t == 0)
    def _():
        c_s[...] = c0_ref[0]
        n_s[...] = n0_ref[0]
        m_s[...] = m0_ref[0]

    ri = lax.broadcasted_iota(jnp.int32, (L, L), 0)
    ci = lax.broadcasted_iota(jnp.int32, (L, L), 1)
    causal = ci <= ri
    tri = causal.astype(F32)
    eye = (ci == ri).astype(F32)

    def to_row(col):
        return jnp.sum(col * eye, axis=0, keepdims=True)

    def chunk(c, carry):
        r0 = pl.multiple_of(c * L, L)
        rows = pl.ds(r0, L)
        g = zg_ref[0, rows, :]
        for hd in range(H_B):
            cols = slice(hd * DK, (hd + 1) * DK)
            q = zqk_ref[0, rows, hd * DK:(hd + 1) * DK] * (DK ** -0.5)
            k = zqk_ref[0, rows, W_B + hd * DK:W_B + (hd + 1) * DK]
            v = zvo_ref[0, rows, hd * DK:(hd + 1) * DK]
            o = zvo_ref[0, rows, W_B + hd * DK:W_B + (hd + 1) * DK]
            ig_c = g[:, hd:hd + 1]
            lf_c = jax.nn.log_sigmoid(g[:, H_B + hd:H_B + hd + 1])
            ig_r = to_row(ig_c)
            b_c = jnp.sum(tri * to_row(lf_c), axis=1, keepdims=True)
            b_r = to_row(b_c)
            m_prev = m_s[hd:hd + 1, 0:1]
            n_prev = n_s[hd:hd + 1, :]
            c_prev = c_s[hd]

            log_d = jnp.where(causal, b_c - b_r + ig_r, -jnp.inf)
            inter = b_c + m_prev
            m_t = jnp.maximum(inter, jnp.max(log_d, axis=1, keepdims=True))
            dmat = jnp.exp(log_d - m_t)
            w_state = jnp.exp(inter - m_t)
            qb = q.astype(BF16)
            s = lax.dot_general(qb, k.astype(BF16), (((1,), (1,)), ((), ())),
                                preferred_element_type=F32) * dmat
            num = (jnp.dot(s.astype(BF16), v.astype(BF16), preferred_element_type=F32)
                   + w_state * jnp.dot(qb, c_prev.astype(BF16), preferred_element_type=F32))
            den = (jnp.sum(s, axis=1, keepdims=True)
                   + w_state * jnp.sum(q * n_prev, axis=1, keepdims=True))
            h = num / jnp.maximum(jnp.abs(den), jnp.exp(-m_t))

            m_new = m_t[L - 1:L, :]
            w_k = jnp.exp(b_c[L - 1:L, :] - b_c + ig_c - m_new)
            w_c = jnp.exp(inter[L - 1:L, :] - m_new)
            kw = k * w_k
            c_s[hd] = w_c * c_prev + jnp.dot(kw.T.astype(BF16), v.astype(BF16),
                                             preferred_element_type=F32)
            n_s[hd:hd + 1, :] = w_c * n_prev + jnp.sum(kw, axis=0, keepdims=True)
            m_s[hd:hd + 1, :] = jnp.broadcast_to(m_new, (1, LANES))

            hn = h * lax.rsqrt(jnp.mean(h * h, axis=1, keepdims=True) + EPS)
            y = jax.nn.sigmoid(o) * (hn * nw_ref[:, cols])
            yb_ref[0, rows, hd * DK:(hd + 1) * DK] = y.astype(yb_ref.dtype)
        return carry

    lax.fori_loop(0, tt // L, chunk, 0)
    cout_ref[0] = c_s[...]
    nout_ref[0] = n_s[...]
    mout_ref[0] = m_s[...]


def _mlstm(z, zg, c0, n0, m0, norm_w, L, tt):
    B, T, _ = z.shape
    state_specs = [pl.BlockSpec((1, H_B, DK, DK), lambda b, t: (b, 0, 0, 0)),
                   pl.BlockSpec((1, H_B, DK), lambda b, t: (b, 0, 0)),
                   pl.BlockSpec((1, H_B, LANES), lambda b, t: (b, 0, 0))]
    return pl.pallas_call(
        functools.partial(_mlstm_kernel, L),
        grid=(B, T // tt),
        in_specs=[pl.BlockSpec((1, tt, 2 * W_B), lambda b, t: (b, t, 1)),
                  pl.BlockSpec((1, tt, 2 * W_B), lambda b, t: (b, t, 2)),
                  pl.BlockSpec((1, tt, LANES), lambda b, t: (b, t, 0))]
        + state_specs + [pl.BlockSpec((1, W_B), lambda b, t: (0, 0))],
        out_specs=[pl.BlockSpec((1, tt, W_B), lambda b, t: (b, t, 0))] + state_specs,
        out_shape=[jax.ShapeDtypeStruct((B, T, W_B), BF16),
                   jax.ShapeDtypeStruct((B, H_B, DK, DK), F32),
                   jax.ShapeDtypeStruct((B, H_B, DK), F32),
                   jax.ShapeDtypeStruct((B, H_B, LANES), F32)],
        scratch_shapes=[pltpu.VMEM((H_B, DK, DK), F32), pltpu.VMEM((H_B, DK), F32),
                        pltpu.VMEM((H_B, LANES), F32)],
        compiler_params=pltpu.CompilerParams(dimension_semantics=("arbitrary", "arbitrary")),
        name="mlstm",
    )(z, z, zg, c0, n0, m0, norm_w)


def _outproj_kernel(x_ref, ya_ref, yb_ref, mod_ref, woa_ref, wob_ref, nw_ref, x1_ref, h2_ref):
    mix = (jnp.dot(ya_ref[0], woa_ref[...], preferred_element_type=F32)
           + jnp.dot(yb_ref[0], wob_ref[...], preferred_element_type=F32))
    x1 = x_ref[0] + mod_ref[0, 2:3, :] * mix
    x1_ref[0] = x1
    h2_ref[0] = _rms(x1, nw_ref[...]) * (1.0 + mod_ref[0, 4:5, :]) + mod_ref[0, 3:4, :]


def _outproj(x, ya, yb, mod, wo_a, wo_b, nw, tm):
    B, T, _ = x.shape
    xspec = pl.BlockSpec((1, tm, D_MODEL), lambda b, t: (b, t, 0))
    yspec = pl.BlockSpec((1, tm, W_A), lambda b, t: (b, t, 0))
    return pl.pallas_call(
        _outproj_kernel,
        grid=(B, T // tm),
        in_specs=[xspec, yspec, yspec,
                  pl.BlockSpec((1, N_ADA, D_MODEL), lambda b, t: (b, 0, 0)),
                  pl.BlockSpec((W_A, D_MODEL), lambda b, t: (0, 0)),
                  pl.BlockSpec((W_B, D_MODEL), lambda b, t: (0, 0)),
                  pl.BlockSpec((1, D_MODEL), lambda b, t: (0, 0))],
        out_specs=[xspec, xspec],
        out_shape=[jax.ShapeDtypeStruct((B, T, D_MODEL), F32)] * 2,
        compiler_params=pltpu.CompilerParams(dimension_semantics=("arbitrary", "arbitrary")),
        name="outproj",
    )(x, ya, yb, mod, wo_a, wo_b, nw)


def _topk_rows(s, ids, k):
    big = jnp.int32(2 ** 30)
    slot = lax.broadcasted_iota(jnp.int32, (k, s.shape[1]), 0)
    vals = jnp.zeros((k, s.shape[1]), F32)
    idxs = jnp.zeros((k, s.shape[1]), jnp.int32)
    for j in range(k):
        m = jnp.max(s, axis=0, keepdims=True)
        i = jnp.min(jnp.where(s == m, ids, big), axis=0, keepdims=True)
        s = jnp.where(ids == i, -jnp.inf, s)
        vals = jnp.where(slot == j, m, vals)
        idxs = jnp.where(slot == j, i, idxs)
    return vals, idxs


def _pair_candidates(v1, v2):
    k, tk = v1.shape
    row = lax.broadcasted_iota(jnp.int32, (SUBLANES, tk), 0)
    sums = [v1[0:1, :] + v2]
    ids = [lax.broadcasted_iota(jnp.int32, (k, tk), 0)]
    for i in range(1, SUBLANES):
        blk = v1[i:i + 1, :] + v2[0:SUBLANES, :]
        sums.append(jnp.where(row < k // (i + 1), blk, -jnp.inf))
        ids.append(row + i * k)
    for i0 in range(SUBLANES, k, SUBLANES):
        sums.append(v1[i0:i0 + SUBLANES, :] + v2[0:1, :])
        ids.append((row + i0) * k)
    return jnp.concatenate(sums, axis=0), jnp.concatenate(ids, axis=0)


def _select_rows(tbl, sel):
    out = jnp.zeros_like(tbl)
    for i in range(tbl.shape[0]):
        out = jnp.where(sel == i, tbl[i:i + 1, :], out)
    return out


def _retrieve_kernel(h2_ref, wq_ref, k1_ref, k2_ref, tix_ref, par_ref, g_ref):
    q = jnp.dot(h2_ref[...].astype(BF16), wq_ref[...], preferred_element_type=F32)
    half = D_QUERY // 2
    nt = (((1,), (1,)), ((), ()))
    key_ids = lax.broadcasted_iota(jnp.int32, (N_KEYS, q.shape[0]), 0)
    for hd in range(PEER_HEADS):
        qh = q[:, hd * D_QUERY:(hd + 1) * D_QUERY]
        qh = qh * lax.rsqrt(jnp.mean(qh * qh, axis=1, keepdims=True) + EPS)
        s1 = lax.dot_general(k1_ref[...], qh[:, :half], nt, preferred_element_type=F32,
                             precision=HIGHEST)
        s2 = lax.dot_general(k2_ref[...], qh[:, half:], nt, preferred_element_type=F32,
                             precision=HIGHEST)
        v1, i1 = _topk_rows(s1, key_ids, PEER_TOPK)
        v2, i2 = _topk_rows(s2, key_ids, PEER_TOPK)
        cand, cand_ids = _pair_candidates(v1, v2)
        sv, si = _topk_rows(cand, cand_ids, PEER_TOPK)
        e1 = _select_rows(i1, lax.shift_right_logical(si, 4))
        e2 = _select_rows(i2, si & (PEER_TOPK - 1))
        expert = e1 * N_KEYS + e2
        ex = jnp.exp(sv - jnp.max(sv, axis=0, keepdims=True))
        rows = slice(hd * PEER_TOPK, (hd + 1) * PEER_TOPK)
        tix_ref[rows, :] = lax.shift_right_logical(expert, 1)
        par_ref[rows, :] = expert & 1
        g_ref[rows, :] = ex / jnp.sum(ex, axis=0, keepdims=True)


def _retrieve(h2, wq, k1, k2, tk):
    n = h2.shape[0]
    ospec = pl.BlockSpec((N_PICK, tk), lambda i: (0, i))
    return pl.pallas_call(
        _retrieve_kernel,
        grid=(n // tk,),
        in_specs=[pl.BlockSpec((tk, D_MODEL), lambda i: (i, 0)),
                  pl.BlockSpec((D_MODEL, PEER_HEADS * D_QUERY), lambda i: (0, 0)),
                  pl.BlockSpec((N_KEYS, D_QUERY // 2), lambda i: (0, 0)),
                  pl.BlockSpec((N_KEYS, D_QUERY // 2), lambda i: (0, 0))],
        out_specs=[ospec, ospec, ospec],
        out_shape=[jax.ShapeDtypeStruct((N_PICK, n), jnp.int32),
                   jax.ShapeDtypeStruct((N_PICK, n), jnp.int32),
                   jax.ShapeDtypeStruct((N_PICK, n), F32)],
        compiler_params=pltpu.CompilerParams(dimension_semantics=("arbitrary",)),
        name="retrieve",
    )(h2, wq, k1, k2)


TILE_ROWS = 2 * SUBLANES
TOKENS_PER_STEP = 4
KG = N_PICK * TILE_ROWS


def _tile_table(tbl):
    return tbl.astype(BF16).reshape(-1, TILE_ROWS, LANES)


def _split_bf16(x):
    hi = x.astype(BF16)
    return hi, (x - hi.astype(F32)).astype(BF16)


def _pick_patterns():
    col = jnp.arange(KG, dtype=jnp.int32)
    pick = jnp.arange(N_PICK, dtype=jnp.int32)
    expand = (col[None, :] // TILE_ROWS == pick[:, None]).astype(BF16)
    half = col[:, None] // SUBLANES
    sum_even = (half == 2 * pick[None, :]).astype(BF16)
    sum_odd = (half == 2 * pick[None, :] + 1).astype(BF16)
    return expand, sum_even, sum_odd


def _peer_act_kernel(tix_ref, x_ref, par_ref, g_ref, sum_even_ref, sum_odd_ref, tbl_ref, w_ref,
                     xa_scr, zs_scr):
    tb = x_ref.shape[0]
    x = x_ref[...]
    xa_scr[...] = jnp.concatenate([x, x - x.astype(BF16).astype(F32)], axis=1).astype(BF16)
    keep = ((lax.broadcasted_iota(jnp.int32, (SUBLANES, KG), 1) & (SUBLANES - 1))
            == lax.broadcasted_iota(jnp.int32, (SUBLANES, KG), 0))
    nt = (((1,), (1,)), ((), ()))

    def one_token(p):
        rows = jnp.concatenate([tbl_ref[tix_ref[p, j]] for j in range(N_PICK)], axis=0)
        res = lax.dot_general(xa_scr[p], rows, nt, preferred_element_type=F32)
        z = jnp.where(keep, res[0:SUBLANES] + res[SUBLANES:], 0.0)
        zs_scr[pl.ds(p, 1), :] = jnp.sum(z, axis=0, keepdims=True)

    def group(i, carry):
        for k in range(TOKENS_PER_STEP):
            one_token(TOKENS_PER_STEP * i + k)
        return carry

    lax.fori_loop(0, tb // TOKENS_PER_STEP, group, 0)

    sum_even = sum_even_ref[...]
    sum_odd = sum_odd_ref[...]
    z_hi, z_lo = _split_bf16(zs_scr[...])
    even = (jnp.dot(z_hi, sum_even, preferred_element_type=F32)
            + jnp.dot(z_lo, sum_even, preferred_element_type=F32))
    odd = (jnp.dot(z_hi, sum_odd, preferred_element_type=F32)
           + jnp.dot(z_lo, sum_odd, preferred_element_type=F32))
    act = jnp.where(par_ref[...] == 1, odd, even)
    w_ref[...] = g_ref[...] * jax.nn.gelu(act)


def _peer_act(tix, x3, par, gates, sum_even, sum_odd, tbl, tb):
    n = tix.shape[0]
    vspec = pl.BlockSpec((tb, N_PICK), lambda i: (i, 0))
    whole = pl.BlockSpec(memory_space=pltpu.VMEM)
    return pl.pallas_call(
        _peer_act_kernel,
        grid=(n // tb,),
        in_specs=[pl.BlockSpec((tb, N_PICK), lambda i: (i, 0), memory_space=pltpu.SMEM),
                  pl.BlockSpec((tb, SUBLANES, LANES), lambda i: (i, 0, 0)),
                  vspec, vspec, whole, whole, whole],
        out_specs=vspec,
        out_shape=jax.ShapeDtypeStruct((n, N_PICK), F32),
        scratch_shapes=[pltpu.VMEM((tb, TILE_ROWS, LANES), BF16), pltpu.VMEM((tb, KG), F32)],
        compiler_params=pltpu.CompilerParams(
            dimension_semantics=("arbitrary",), vmem_limit_bytes=PEER_VMEM_LIMIT),
        name="peer_act",
    )(tix, x3, par, gates, sum_even, sum_odd, tbl)


def _peer_out_kernel(tix_ref, w_ref, par_ref, x1_ref, g2_ref, fw_ref, expand_ref, tbl_ref, y_ref,
                     wexp_scr):
    tb = x1_ref.shape[0]

    expand = expand_ref[...]
    w_hi, w_lo = _split_bf16(w_ref[...])
    par8 = (par_ref[...] * SUBLANES).astype(F32).astype(BF16)
    wexp_scr[0:tb, :] = jnp.dot(w_hi, expand, preferred_element_type=F32)
    wexp_scr[tb:2 * tb, :] = jnp.dot(w_lo, expand, preferred_element_type=F32)
    wexp_scr[2 * tb:3 * tb, :] = jnp.dot(par8, expand, preferred_element_type=F32)

    rel = ((lax.broadcasted_iota(jnp.int32, (SUBLANES, KG), 1) & (TILE_ROWS - 1))
           - lax.broadcasted_iota(jnp.int32, (SUBLANES, KG), 0)).astype(F32)

    def one_token(p):
        sel = rel == wexp_scr[pl.ds(2 * tb + p, 1), :]
        a_hi = jnp.where(sel, wexp_scr[pl.ds(p, 1), :], 0.0)
        a_lo = jnp.where(sel, wexp_scr[pl.ds(tb + p, 1), :], 0.0)
        a = jnp.concatenate([a_hi, a_lo], axis=0).astype(BF16)
        rows = jnp.concatenate([tbl_ref[tix_ref[p, j]] for j in range(N_PICK)], axis=0)
        res = jnp.dot(a, rows, preferred_element_type=F32)
        y_ref[p] = res[0:SUBLANES] + res[SUBLANES:]

    def group(i, carry):
        for k in range(TOKENS_PER_STEP):
            one_token(TOKENS_PER_STEP * i + k)
        return carry

    lax.fori_loop(0, tb // TOKENS_PER_STEP, group, 0)

    x2 = x1_ref[...] + g2_ref[...] * y_ref[...]
    ms = jnp.sum(jnp.sum(x2 * x2, axis=2, keepdims=True), axis=1, keepdims=True) / D_MODEL
    y_ref[...] = x2 * lax.rsqrt(ms + EPS) * fw_ref[...]


def _peer_out(tix, w, par, x1_3, g2_3, fw_3, expand, tbl, tb, tokens_per_batch):
    n = tix.shape[0]
    vspec = pl.BlockSpec((tb, N_PICK), lambda i: (i, 0))
    xspec = pl.BlockSpec((tb, SUBLANES, LANES), lambda i: (i, 0, 0))
    whole = pl.BlockSpec(memory_space=pltpu.VMEM)
    return pl.pallas_call(
        _peer_out_kernel,
        grid=(n // tb,),
        in_specs=[pl.BlockSpec((tb, N_PICK), lambda i: (i, 0), memory_space=pltpu.SMEM),
                  vspec, vspec, xspec,
                  pl.BlockSpec((1, SUBLANES, LANES), lambda i: ((i * tb) // tokens_per_batch, 0, 0)),
                  pl.BlockSpec((SUBLANES, LANES), lambda i: (0, 0)),
                  whole, whole],
        out_specs=xspec,
        out_shape=jax.ShapeDtypeStruct((n, SUBLANES, LANES), F32),
        scratch_shapes=[pltpu.VMEM((3 * tb, KG), F32)],
        compiler_params=pltpu.CompilerParams(
            dimension_semantics=("arbitrary",), vmem_limit_bytes=PEER_VMEM_LIMIT),
        name="peer_out",
    )(tix, w, par, x1_3, g2_3, fw_3, expand, tbl)


def _trunk(x, mod, h0, conv0, c0, n0, m0, chunk_len, p):
    B, T, _ = x.shape
    n = B * T
    tm = min(512, T)
    z, zg = _inproj(x, mod, p["norm1_w"], p["w_main"], p["b_main"], p["w_gate"], p["b_gate"], tm)
    ya, h_last, conv_new = _rglru(z, h0.reshape(B, 1, W_A), conv0, p["conv_w"], p["conv_b"],
                                  p["w_ri"], p["b_ri"], p["lam"], tm)
    m0b = jnp.broadcast_to(m0[:, :, None], (B, H_B, LANES))
    yb, c_new, n_new, m_new = _mlstm(z, zg, c0, n0, m0b, p["mlstm_norm_w"], chunk_len, tm)
    x1, h2 = _outproj(x, ya, yb, mod, p["wo_a"], p["wo_b"], p["norm2_w"], tm)

    tk = 256
    tix_t, par_t, g_t = _retrieve(h2.reshape(n, D_MODEL), p["wq"], p["k1"], p["k2"], tk)
    tix = tix_t.T
    par = par_t.T
    tb = 32
    expand, sum_even, sum_odd = _pick_patterns()
    w = _peer_act(tix, h2.reshape(n, SUBLANES, LANES), par, g_t.T, sum_even, sum_odd,
                  p["u_tbl"], tb)
    y3 = _peer_out(tix, w, par, x1.reshape(n, SUBLANES, LANES),
                   mod[:, 5].reshape(B, SUBLANES, LANES), p["final_w"], expand, p["v_tbl"], tb, T)
    return (y3.reshape(B, T, D_MODEL), h_last.reshape(B, W_A), conv_new, c_new, n_new,
            m_new[:, :, 0])


def kernel(x_prompt, x_sample, c_prompt, c_sample, state_lru_h, state_lru_conv, state_mlstm_C,
           state_mlstm_n, state_mlstm_m, norm1_w, ada_w, ada_b, w_in, b_in, conv_w, conv_b,
           lru_wr, lru_br, lru_wi, lru_bi, lru_lambda, mlstm_norm_w, w_out, norm2_w,
           peer_wq, peer_k1, peer_k2, peer_u, peer_v, final_norm_w):
    depth = w_in.shape[0]
    assert depth == 1, "single-layer trunk only"
    B = x_prompt.shape[0]
    Bs = x_sample.shape[0]
    l = 0

    def block_diag(w):
        nb, c, d = w.shape
        eye = jnp.eye(nb, dtype=w.dtype)
        return jnp.einsum("ncd,nm->ncmd", w, eye).reshape(nb * c, nb * d)

    pad_g = LANES - N_GATE
    p = {
        "norm1_w": norm1_w[l].reshape(1, D_MODEL),
        "w_main": w_in[l][:, :N_MAIN].astype(BF16),
        "b_main": b_in[l][:N_MAIN].reshape(1, N_MAIN),
        "w_gate": jnp.pad(w_in[l][:, N_MAIN:], ((0, 0), (0, pad_g))),
        "b_gate": jnp.pad(b_in[l][N_MAIN:], (0, pad_g)).reshape(1, LANES),
        "conv_w": conv_w[l],
        "conv_b": conv_b[l].reshape(1, W_A),
        "w_ri": jnp.concatenate([block_diag(lru_wr[l]), block_diag(lru_wi[l])], axis=1).astype(BF16),
        "b_ri": jnp.concatenate([lru_br[l], lru_bi[l]]).reshape(1, 2 * W_A),
        "lam": lru_lambda[l].reshape(1, W_A),
        "mlstm_norm_w": mlstm_norm_w[l].reshape(1, W_B),
        "wo_a": w_out[l][:W_A].astype(BF16),
        "wo_b": w_out[l][W_A:].astype(BF16),
        "norm2_w": norm2_w[l].reshape(1, D_MODEL),
        "wq": peer_wq[l].astype(BF16),
        "k1": peer_k1[l],
        "k2": peer_k2[l],
        "u_tbl": _tile_table(peer_u[l]),
        "v_tbl": _tile_table(peer_v[l]),
        "final_w": final_norm_w.reshape(SUBLANES, LANES),
    }

    mod = _adaln(jnp.concatenate([c_prompt, c_sample], axis=0), ada_w[l], ada_b[l])
    mod = mod.reshape(B + Bs, N_ADA, D_MODEL)

    dt = x_prompt.dtype
    outs_p = _trunk(x_prompt, mod[:B], jnp.zeros((B, W_A), dt), jnp.zeros((B, CONV_W - 1, W_A), dt),
                    jnp.zeros((B, H_B, DK, DK), dt), jnp.zeros((B, H_B, DK), dt),
                    jnp.zeros((B, H_B), dt), CHUNK, p)
    outs_s = _trunk(x_sample, mod[B:], state_lru_h[l], state_lru_conv[l], state_mlstm_C[l],
                    state_mlstm_n[l], state_mlstm_m[l], x_sample.shape[1], p)
    y_p, *st_p = outs_p
    y_s, *st_s = outs_s
    st_p = [s[None].astype(r.dtype) for s, r in zip(
        st_p, (state_lru_h, state_lru_conv, state_mlstm_C, state_mlstm_n, state_mlstm_m))]
    st_s = [s[None].astype(r.dtype) for s, r in zip(
        st_s, (state_lru_h, state_lru_conv, state_mlstm_C, state_mlstm_n, state_mlstm_m))]
    return (y_p, y_s, *st_p, *st_s)
```

```python
import functools

import jax
import jax.numpy as jnp
from jax import lax
from jax.experimental import pallas as pl
from jax.experimental.pallas import tpu as pltpu

D_MODEL = 1024
W_A = 512
W_B = 512
LRU_BLOCKS = 8
CONV_W = 4
LRU_C = 8.0
H_B = 4
DK = W_B // H_B
N_ADA = 6
PEER_HEADS = 8
N_KEYS = 128
D_QUERY = 256
PEER_TOPK = 16
N_PICK = PEER_HEADS * PEER_TOPK
EPS = 1e-6
CHUNK = 64
N_MAIN = 2 * W_A + 4 * W_B
N_GATE = 2 * H_B

LANES = 128
SUBLANES = 8
PEER_VMEM_LIMIT = 48 * 1024 * 1024

F32 = jnp.float32
BF16 = jnp.bfloat16
HIGHEST = lax.Precision.HIGHEST


def _rms(x, w):
    return x * lax.rsqrt(jnp.mean(x * x, axis=-1, keepdims=True) + EPS) * w


def _adaln_kernel(c_ref, w_ref, b_ref, o_ref):
    s = jax.nn.silu(c_ref[...])
    o_ref[...] = jnp.dot(s, w_ref[...], preferred_element_type=F32, precision=HIGHEST) + b_ref[...]


def _adaln(c, w, b):
    nb = c.shape[0]
    n_out = w.shape[1]
    tn = 1024
    return pl.pallas_call(
        _adaln_kernel,
        grid=(n_out // tn,),
        in_specs=[pl.BlockSpec((nb, D_MODEL), lambda j: (0, 0)),
                  pl.BlockSpec((D_MODEL, tn), lambda j: (0, j)),
                  pl.BlockSpec((1, tn), lambda j: (0, j))],
        out_specs=pl.BlockSpec((nb, tn), lambda j: (0, j)),
        out_shape=jax.ShapeDtypeStruct((nb, n_out), F32),
        name="adaln",
    )(c, w, b.reshape(1, n_out))


def _inproj_kernel(x_ref, mod_ref, nw_ref, w_ref, b_ref, wg_ref, bg_ref, z_ref, zg_ref):
    h = _rms(x_ref[0], nw_ref[...]) * (1.0 + mod_ref[0, 1:2, :]) + mod_ref[0, 0:1, :]
    z_ref[0] = jnp.dot(h.astype(BF16), w_ref[...], preferred_element_type=F32) + b_ref[...]
    zg_ref[0] = jnp.dot(h, wg_ref[...], preferred_element_type=F32, precision=HIGHEST) + bg_ref[...]


def _inproj(x, mod, nw, w_main, b_main, w_gate, b_gate, tm):
    B, T, _ = x.shape
    return pl.pallas_call(
        _inproj_kernel,
        grid=(B, T // tm),
        in_specs=[pl.BlockSpec((1, tm, D_MODEL), lambda b, t: (b, t, 0)),
                  pl.BlockSpec((1, N_ADA, D_MODEL), lambda b, t: (b, 0, 0)),
                  pl.BlockSpec((1, D_MODEL), lambda b, t: (0, 0)),
                  pl.BlockSpec((D_MODEL, N_MAIN), lambda b, t: (0, 0)),
                  pl.BlockSpec((1, N_MAIN), lambda b, t: (0, 0)),
                  pl.BlockSpec((D_MODEL, LANES), lambda b, t: (0, 0)),
                  pl.BlockSpec((1, LANES), lambda b, t: (0, 0))],
        out_specs=[pl.BlockSpec((1, tm, N_MAIN), lambda b, t: (b, t, 0)),
                   pl.BlockSpec((1, tm, LANES), lambda b, t: (b, t, 0))],
        out_shape=[jax.ShapeDtypeStruct((B, T, N_MAIN), F32),
                   jax.ShapeDtypeStruct((B, T, LANES), F32)],
        compiler_params=pltpu.CompilerParams(
            dimension_semantics=("arbitrary", "arbitrary"), vmem_limit_bytes=48 * 1024 * 1024),
        name="inproj",
    )(x, mod, nw, w_main, b_main, w_gate, b_gate)


def _rglru_kernel(z_ref, h0_ref, conv0_ref, cw_ref, cb_ref, wri_ref, bri_ref, lam_ref,
                  ya_ref, hlast_ref, convnew_ref, xbuf, hcar):
    t = pl.program_id(1)
    tt = z_ref.shape[1]

    @pl.when(t == 0)
    def _():
        xbuf[0:SUBLANES, :] = jnp.zeros((SUBLANES, W_A), F32)
        xbuf[SUBLANES - (CONV_W - 1):SUBLANES, :] = conv0_ref[0]
        hcar[...] = h0_ref[0]

    xbuf[SUBLANES:SUBLANES + tt, :] = z_ref[0, :, 0:W_A]
    u = cb_ref[...]
    for k in range(CONV_W):
        off = SUBLANES - (CONV_W - 1) + k
        u = u + xbuf[off:off + tt, :] * cw_ref[k:k + 1, :]
    convnew_ref[0] = xbuf[tt + SUBLANES - (CONV_W - 1):tt + SUBLANES, :]
    xbuf[0:SUBLANES, :] = xbuf[tt:tt + SUBLANES, :]

    ri = jnp.dot(u.astype(BF16), wri_ref[...], preferred_element_type=F32) + bri_ref[...]
    r = jax.nn.sigmoid(ri[:, 0:W_A])
    i = jax.nn.sigmoid(ri[:, W_A:2 * W_A])
    log_a = (-LRU_C) * r * jax.nn.softplus(-lam_ref[...])
    a = jnp.exp(log_a)
    bx = jnp.sqrt(jnp.tanh(-log_a) * (a * a + 1.0)) * (i * u)

    row = lax.broadcasted_iota(jnp.int32, (tt, W_A), 0)
    d = 1
    while d < tt:
        keep = row >= d
        a_sh = jnp.where(keep, pltpu.roll(a, d, 0), 1.0)
        b_sh = jnp.where(keep, pltpu.roll(bx, d, 0), 0.0)
        bx = bx + a * b_sh
        a = a * a_sh
        d *= 2
    h = bx + a * hcar[...]
    hcar[...] = h[tt - 1:tt, :]
    hlast_ref[0] = h[tt - 1:tt, :]
    ya_ref[0] = (h * jax.nn.gelu(z_ref[0, :, W_A:2 * W_A])).astype(ya_ref.dtype)


def _rglru(z, h0, conv0, conv_w, conv_b, w_ri, b_ri, lam, tt):
    B, T, _ = z.shape
    return pl.pallas_call(
        _rglru_kernel,
        grid=(B, T // tt),
        in_specs=[pl.BlockSpec((1, tt, 2 * W_A), lambda b, t: (b, t, 0)),
                  pl.BlockSpec((1, 1, W_A), lambda b, t: (b, 0, 0)),
                  pl.BlockSpec((1, CONV_W - 1, W_A), lambda b, t: (b, 0, 0)),
                  pl.BlockSpec((CONV_W, W_A), lambda b, t: (0, 0)),
                  pl.BlockSpec((1, W_A), lambda b, t: (0, 0)),
                  pl.BlockSpec((W_A, 2 * W_A), lambda b, t: (0, 0)),
                  pl.BlockSpec((1, 2 * W_A), lambda b, t: (0, 0)),
                  pl.BlockSpec((1, W_A), lambda b, t: (0, 0))],
        out_specs=[pl.BlockSpec((1, tt, W_A), lambda b, t: (b, t, 0)),
                   pl.BlockSpec((1, 1, W_A), lambda b, t: (b, 0, 0)),
                   pl.BlockSpec((1, CONV_W - 1, W_A), lambda b, t: (b, 0, 0))],
        out_shape=[jax.ShapeDtypeStruct((B, T, W_A), BF16),
                   jax.ShapeDtypeStruct((B, 1, W_A), F32),
                   jax.ShapeDtypeStruct((B, CONV_W - 1, W_A), F32)],
        scratch_shapes=[pltpu.VMEM((tt + SUBLANES, W_A), F32), pltpu.VMEM((1, W_A), F32)],
        compiler_params=pltpu.CompilerParams(dimension_semantics=("arbitrary", "arbitrary")),
        name="rglru",
    )(z, h0, conv0, conv_w, conv_b, w_ri, b_ri, lam)


def _mlstm_kernel(L, zqk_ref, zvo_ref, zg_ref, c0_ref, n0_ref, m0_ref, nw_ref,
                  yb_ref, cout_ref, nout_ref, mout_ref, c_s, n_s, m_s):
    t = pl.program_id(1)
    nb, tt, _ = zqk_ref.shape

    @pl.when(t == 0)
    def _():
        c_s[...] = c0_ref[...]
        n_s[...] = n0_ref[...]
        m_s[...] = m0_ref[...]

    ri = lax.broadcasted_iota(jnp.int32, (L, L), 0)
    ci = lax.broadcasted_iota(jnp.int32, (L, L), 1)
    causal = ci <= ri
    tri = causal.astype(F32)
    eye = (ci == ri).astype(F32)

    def to_row(col):
        return jnp.sum(col * eye, axis=0, keepdims=True)

    def chunk(c, carry):
        r0 = pl.multiple_of(c * L, L)
        rows = pl.ds(r0, L)
        for bi, hd in [(bi, hd) for bi in range(nb) for hd in range(H_B)]:
            g = zg_ref[bi, rows, :]
            cols = slice(hd * DK, (hd + 1) * DK)
            q = zqk_ref[bi, rows, hd * DK:(hd + 1) * DK] * (DK ** -0.5)
            k = zqk_ref[bi, rows, W_B + hd * DK:W_B + (hd + 1) * DK]
            v = zvo_ref[bi, rows, hd * DK:(hd + 1) * DK]
            o = zvo_ref[bi, rows, W_B + hd * DK:W_B + (hd + 1) * DK]
            ig_c = g[:, hd:hd + 1]
            lf_c = jax.nn.log_sigmoid(g[:, H_B + hd:H_B + hd + 1])
            ig_r = to_row(ig_c)
            b_c = jnp.sum(tri * to_row(lf_c), axis=1, keepdims=True)
            b_r = to_row(b_c)
            m_prev = m_s[bi, hd:hd + 1, 0:1]
            n_prev = n_s[bi, hd:hd + 1, :]
            c_prev = c_s[bi, hd]

            log_d = jnp.where(causal, b_c - b_r + ig_r, -jnp.inf)
            inter = b_c + m_prev
            m_t = jnp.maximum(inter, jnp.max(log_d, axis=1, keepdims=True))
            dmat = jnp.exp(log_d - m_t)
            w_state = jnp.exp(inter - m_t)
            qb = q.astype(BF16)
            s = lax.dot_general(qb, k.astype(BF16), (((1,), (1,)), ((), ())),
                                preferred_element_type=F32) * dmat
            num = (jnp.dot(s.astype(BF16), v.astype(BF16), preferred_element_type=F32)
                   + w_state * jnp.dot(qb, c_prev.astype(BF16), preferred_element_type=F32))
            den = (jnp.sum(s, axis=1, keepdims=True)
                   + w_state * jnp.sum(q * n_prev, axis=1, keepdims=True))
            h = num / jnp.maximum(jnp.abs(den), jnp.exp(-m_t))

            m_new = m_t[L - 1:L, :]
            w_k = jnp.exp(b_c[L - 1:L, :] - b_c + ig_c - m_new)
            w_c = jnp.exp(inter[L - 1:L, :] - m_new)
            kw = k * w_k
            c_s[bi, hd] = w_c * c_prev + jnp.dot(kw.T.astype(BF16), v.astype(BF16),
                                                 preferred_element_type=F32)
            n_s[bi, hd:hd + 1, :] = w_c * n_prev + jnp.sum(kw, axis=0, keepdims=True)
            m_s[bi, hd:hd + 1, :] = jnp.broadcast_to(m_new, (1, LANES))

            hn = h * lax.rsqrt(jnp.mean(h * h, axis=1, keepdims=True) + EPS)
            y = jax.nn.sigmoid(o) * (hn * nw_ref[:, cols])
            yb_ref[bi, rows, hd * DK:(hd + 1) * DK] = y.astype(yb_ref.dtype)
        return carry

    lax.fori_loop(0, tt // L, chunk, 0)
    cout_ref[...] = c_s[...]
    nout_ref[...] = n_s[...]
    mout_ref[...] = m_s[...]


def _mlstm(z, zg, c0, n0, m0, norm_w, L, tt, nb):
    B, T, _ = z.shape
    state_specs = [pl.BlockSpec((nb, H_B, DK, DK), lambda b, t: (b, 0, 0, 0)),
                   pl.BlockSpec((nb, H_B, DK), lambda b, t: (b, 0, 0)),
                   pl.BlockSpec((nb, H_B, LANES), lambda b, t: (b, 0, 0))]
    return pl.pallas_call(
        functools.partial(_mlstm_kernel, L),
        grid=(B // nb, T // tt),
        in_specs=[pl.BlockSpec((nb, tt, 2 * W_B), lambda b, t: (b, t, 1)),
                  pl.BlockSpec((nb, tt, 2 * W_B), lambda b, t: (b, t, 2)),
                  pl.BlockSpec((nb, tt, LANES), lambda b, t: (b, t, 0))]
        + state_specs + [pl.BlockSpec((1, W_B), lambda b, t: (0, 0))],
        out_specs=[pl.BlockSpec((nb, tt, W_B), lambda b, t: (b, t, 0))] + state_specs,
        out_shape=[jax.ShapeDtypeStruct((B, T, W_B), BF16),
                   jax.ShapeDtypeStruct((B, H_B, DK, DK), F32),
                   jax.ShapeDtypeStruct((B, H_B, DK), F32),
                   jax.ShapeDtypeStruct((B, H_B, LANES), F32)],
        scratch_shapes=[pltpu.VMEM((nb, H_B, DK, DK), F32), pltpu.VMEM((nb, H_B, DK), F32),
                        pltpu.VMEM((nb, H_B, LANES), F32)],
        compiler_params=pltpu.CompilerParams(dimension_semantics=("arbitrary", "arbitrary")),
        name="mlstm",
    )(z, z, zg, c0, n0, m0, norm_w)


def _outproj_kernel(x_ref, ya_ref, yb_ref, mod_ref, woa_ref, wob_ref, nw_ref, x1_ref, h2_ref):
    mix = (jnp.dot(ya_ref[0], woa_ref[...], preferred_element_type=F32)
           + jnp.dot(yb_ref[0], wob_ref[...], preferred_element_type=F32))
    x1 = x_ref[0] + mod_ref[0, 2:3, :] * mix
    x1_ref[0] = x1
    h2_ref[0] = _rms(x1, nw_ref[...]) * (1.0 + mod_ref[0, 4:5, :]) + mod_ref[0, 3:4, :]


def _outproj(x, ya, yb, mod, wo_a, wo_b, nw, tm):
    B, T, _ = x.shape
    xspec = pl.BlockSpec((1, tm, D_MODEL), lambda b, t: (b, t, 0))
    yspec = pl.BlockSpec((1, tm, W_A), lambda b, t: (b, t, 0))
    return pl.pallas_call(
        _outproj_kernel,
        grid=(B, T // tm),
        in_specs=[xspec, yspec, yspec,
                  pl.BlockSpec((1, N_ADA, D_MODEL), lambda b, t: (b, 0, 0)),
                  pl.BlockSpec((W_A, D_MODEL), lambda b, t: (0, 0)),
                  pl.BlockSpec((W_B, D_MODEL), lambda b, t: (0, 0)),
                  pl.BlockSpec((1, D_MODEL), lambda b, t: (0, 0))],
        out_specs=[xspec, xspec],
        out_shape=[jax.ShapeDtypeStruct((B, T, D_MODEL), F32)] * 2,
        compiler_params=pltpu.CompilerParams(dimension_semantics=("arbitrary", "arbitrary")),
        name="outproj",
    )(x, ya, yb, mod, wo_a, wo_b, nw)


def _topk_rows(s, ids, k):
    ids = ids.astype(F32)
    big = jnp.float32(2 ** 24)
    slot = lax.broadcasted_iota(jnp.int32, (k, s.shape[1]), 0)
    vals = jnp.zeros((k, s.shape[1]), F32)
    idxs = jnp.zeros((k, s.shape[1]), F32)
    for j in range(k):
        m = jnp.max(s, axis=0, keepdims=True)
        i = jnp.min(jnp.where(s == m, ids, big), axis=0, keepdims=True)
        s = jnp.where(ids == i, -jnp.inf, s)
        vals = jnp.where(slot == j, m, vals)
        idxs = jnp.where(slot == j, i, idxs)
    return vals, idxs.astype(jnp.int32)


def _pair_candidates(v1, v2):
    k, tk = v1.shape
    row = lax.broadcasted_iota(jnp.int32, (SUBLANES, tk), 0)
    sums = [v1[0:1, :] + v2]
    ids = [lax.broadcasted_iota(jnp.int32, (k, tk), 0)]
    for i in range(1, SUBLANES):
        blk = v1[i:i + 1, :] + v2[0:SUBLANES, :]
        sums.append(jnp.where(row < k // (i + 1), blk, -jnp.inf))
        ids.append(row + i * k)
    for i0 in range(SUBLANES, k, SUBLANES):
        sums.append(v1[i0:i0 + SUBLANES, :] + v2[0:1, :])
        ids.append((row + i0) * k)
    return jnp.concatenate(sums, axis=0), jnp.concatenate(ids, axis=0)


def _select_rows(tbl, sel):
    out = jnp.zeros_like(tbl)
    for i in range(tbl.shape[0]):
        out = jnp.where(sel == i, tbl[i:i + 1, :], out)
    return out


def _retrieve_kernel(h2_ref, wq_ref, k1_ref, k2_ref, tix_ref, par_ref, g_ref):
    q = jnp.dot(h2_ref[...].astype(BF16), wq_ref[...], preferred_element_type=F32)
    half = D_QUERY // 2
    nt = (((1,), (1,)), ((), ()))
    key_ids = lax.broadcasted_iota(jnp.int32, (N_KEYS, q.shape[0]), 0).astype(F32)
    for hd in range(PEER_HEADS):
        qh = q[:, hd * D_QUERY:(hd + 1) * D_QUERY]
        qh = qh * lax.rsqrt(jnp.mean(qh * qh, axis=1, keepdims=True) + EPS)
        s1 = lax.dot_general(k1_ref[...], qh[:, :half], nt, preferred_element_type=F32,
                             precision=HIGHEST)
        s2 = lax.dot_general(k2_ref[...], qh[:, half:], nt, preferred_element_type=F32,
                             precision=HIGHEST)
        v1, i1 = _topk_rows(s1, key_ids, PEER_TOPK)
        v2, i2 = _topk_rows(s2, key_ids, PEER_TOPK)
        cand, cand_ids = _pair_candidates(v1, v2)
        sv, si = _topk_rows(cand, cand_ids, PEER_TOPK)
        e1 = _select_rows(i1, lax.shift_right_logical(si, 4))
        e2 = _select_rows(i2, si & (PEER_TOPK - 1))
        expert = e1 * N_KEYS + e2
        ex = jnp.exp(sv - jnp.max(sv, axis=0, keepdims=True))
        rows = slice(hd * PEER_TOPK, (hd + 1) * PEER_TOPK)
        tix_ref[rows, :] = lax.shift_right_logical(expert, 1) * SUBLANES
        par_ref[rows, :] = expert & 1
        g_ref[rows, :] = ex / jnp.sum(ex, axis=0, keepdims=True)


def _retrieve(h2, wq, k1, k2, tk):
    n = h2.shape[0]
    ospec = pl.BlockSpec((N_PICK, tk), lambda i: (0, i))
    return pl.pallas_call(
        _retrieve_kernel,
        grid=(n // tk,),
        in_specs=[pl.BlockSpec((tk, D_MODEL), lambda i: (i, 0)),
                  pl.BlockSpec((D_MODEL, PEER_HEADS * D_QUERY), lambda i: (0, 0)),
                  pl.BlockSpec((N_KEYS, D_QUERY // 2), lambda i: (0, 0)),
                  pl.BlockSpec((N_KEYS, D_QUERY // 2), lambda i: (0, 0))],
        out_specs=[ospec, ospec, ospec],
        out_shape=[jax.ShapeDtypeStruct((N_PICK, n), jnp.int32),
                   jax.ShapeDtypeStruct((N_PICK, n), jnp.int32),
                   jax.ShapeDtypeStruct((N_PICK, n), F32)],
        compiler_params=pltpu.CompilerParams(dimension_semantics=("arbitrary",)),
        name="retrieve",
    )(h2, wq, k1, k2)


TILE_ROWS = 2 * SUBLANES
TOKENS_PER_STEP = 8
KG = N_PICK * TILE_ROWS


def _tile_table(tbl):
    halves = lax.bitcast_convert_type(tbl.astype(BF16), jnp.uint16).astype(jnp.uint32)
    halves = halves.reshape(-1, SUBLANES, 2, LANES)
    words = halves[:, :, 0, :] | (halves[:, :, 1, :] << 16)
    return words.reshape(-1, LANES)


def _gathered_rows(tbl_ref, off_ref, p):
    tiles = []
    for j in range(N_PICK):
        off = pl.multiple_of(off_ref[p, j], SUBLANES)
        tiles.append(pltpu.bitcast(tbl_ref[pl.ds(off, SUBLANES), :], BF16))
    return jnp.concatenate(tiles, axis=0)


def _split_bf16(x):
    hi = x.astype(BF16)
    return hi, (x - hi.astype(F32)).astype(BF16)


def _pick_patterns():
    col = jnp.arange(KG, dtype=jnp.int32)
    pick = jnp.arange(N_PICK, dtype=jnp.int32)
    expand = (col[None, :] // TILE_ROWS == pick[:, None]).astype(BF16)
    half = col[:, None] // SUBLANES
    sum_even = (half == 2 * pick[None, :]).astype(BF16)
    sum_odd = (half == 2 * pick[None, :] + 1).astype(BF16)
    return expand, sum_even, sum_odd


def _peer_act_kernel(tix_ref, x_ref, par_ref, g_ref, sum_even_ref, sum_odd_ref, tbl_ref, w_ref,
                     xa_scr, zs_scr):
    tb = x_ref.shape[0]
    x = x_ref[...]
    xa_scr[...] = jnp.concatenate([x, x - x.astype(BF16).astype(F32)], axis=1).astype(BF16)
    keep = ((lax.broadcasted_iota(jnp.int32, (SUBLANES, KG), 1) & (SUBLANES - 1))
            == lax.broadcasted_iota(jnp.int32, (SUBLANES, KG), 0))
    nt = (((1,), (1,)), ((), ()))

    def one_token(p):
        rows = _gathered_rows(tbl_ref, tix_ref, p)
        res = lax.dot_general(xa_scr[p], rows, nt, preferred_element_type=F32)
        z = jnp.where(keep, res[0:SUBLANES] + res[SUBLANES:], 0.0)
        zs_scr[pl.ds(p, 1), :] = jnp.sum(z, axis=0, keepdims=True)

    def group(i, carry):
        for k in range(TOKENS_PER_STEP):
            one_token(TOKENS_PER_STEP * i + k)
        return carry

    lax.fori_loop(0, tb // TOKENS_PER_STEP, group, 0)

    sum_even = sum_even_ref[...]
    sum_odd = sum_odd_ref[...]
    z_hi, z_lo = _split_bf16(zs_scr[...])
    even = (jnp.dot(z_hi, sum_even, preferred_element_type=F32)
            + jnp.dot(z_lo, sum_even, preferred_element_type=F32))
    odd = (jnp.dot(z_hi, sum_odd, preferred_element_type=F32)
           + jnp.dot(z_lo, sum_odd, preferred_element_type=F32))
    act = jnp.where(par_ref[...] == 1, odd, even)
    w_ref[...] = g_ref[...] * jax.nn.gelu(act)


def _peer_act(tix, x3, par, gates, sum_even, sum_odd, tbl, tb):
    n = tix.shape[0]
    vspec = pl.BlockSpec((tb, N_PICK), lambda i: (i, 0))
    whole = pl.BlockSpec(memory_space=pltpu.VMEM)
    return pl.pallas_call(
        _peer_act_kernel,
        grid=(n // tb,),
        in_specs=[pl.BlockSpec((tb, N_PICK), lambda i: (i, 0), memory_space=pltpu.SMEM),
                  pl.BlockSpec((tb, SUBLANES, LANES), lambda i: (i, 0, 0)),
                  vspec, vspec, whole, whole, whole],
        out_specs=vspec,
        out_shape=jax.ShapeDtypeStruct((n, N_PICK), F32),
        scratch_shapes=[pltpu.VMEM((tb, TILE_ROWS, LANES), BF16), pltpu.VMEM((tb, KG), F32)],
        compiler_params=pltpu.CompilerParams(
            dimension_semantics=("arbitrary",), vmem_limit_bytes=PEER_VMEM_LIMIT),
        name="peer_act",
    )(tix, x3, par, gates, sum_even, sum_odd, tbl)


def _peer_out_kernel(tix_ref, w_ref, par_ref, x1_ref, g2_ref, fw_ref, expand_ref, tbl_ref, y_ref,
                     wexp_scr):
    tb = x1_ref.shape[0]

    expand = expand_ref[...]
    w_hi, w_lo = _split_bf16(w_ref[...])
    par8 = (par_ref[...] * SUBLANES).astype(F32).astype(BF16)
    wexp_scr[0:tb, :] = jnp.dot(w_hi, expand, preferred_element_type=F32)
    wexp_scr[tb:2 * tb, :] = jnp.dot(w_lo, expand, preferred_element_type=F32)
    wexp_scr[2 * tb:3 * tb, :] = jnp.dot(par8, expand, preferred_element_type=F32)

    rel = ((lax.broadcasted_iota(jnp.int32, (SUBLANES, KG), 1) & (TILE_ROWS - 1))
           - lax.broadcasted_iota(jnp.int32, (SUBLANES, KG), 0)).astype(F32)

    def one_token(p):
        sel = rel == wexp_scr[pl.ds(2 * tb + p, 1), :]
        a_hi = jnp.where(sel, wexp_scr[pl.ds(p, 1), :], 0.0)
        a_lo = jnp.where(sel, wexp_scr[pl.ds(tb + p, 1), :], 0.0)
        a = jnp.concatenate([a_hi, a_lo], axis=0).astype(BF16)
        rows = _gathered_rows(tbl_ref, tix_ref, p)
        res = jnp.dot(a, rows, preferred_element_type=F32)
        y_ref[p] = res[0:SUBLANES] + res[SUBLANES:]

    def group(i, carry):
        for k in range(TOKENS_PER_STEP):
            one_token(TOKENS_PER_STEP * i + k)
        return carry

    lax.fori_loop(0, tb // TOKENS_PER_STEP, group, 0)

    x2 = x1_ref[...] + g2_ref[...] * y_ref[...]
    ms = jnp.sum(jnp.sum(x2 * x2, axis=2, keepdims=True), axis=1, keepdims=True) / D_MODEL
    y_ref[...] = x2 * lax.rsqrt(ms + EPS) * fw_ref[...]


def _peer_out(tix, w, par, x1_3, g2_3, fw_3, expand, tbl, tb, tokens_per_batch):
    n = tix.shape[0]
    vspec = pl.BlockSpec((tb, N_PICK), lambda i: (i, 0))
    xspec = pl.BlockSpec((tb, SUBLANES, LANES), lambda i: (i, 0, 0))
    whole = pl.BlockSpec(memory_space=pltpu.VMEM)
    return pl.pallas_call(
        _peer_out_kernel,
        grid=(n // tb,),
        in_specs=[pl.BlockSpec((tb, N_PICK), lambda i: (i, 0), memory_space=pltpu.SMEM),
                  vspec, vspec, xspec,
                  pl.BlockSpec((1, SUBLANES, LANES), lambda i: ((i * tb) // tokens_per_batch, 0, 0)),
                  pl.BlockSpec((SUBLANES, LANES), lambda i: (0, 0)),
                  whole, whole],
        out_specs=xspec,
        out_shape=jax.ShapeDtypeStruct((n, SUBLANES, LANES), F32),
        scratch_shapes=[pltpu.VMEM((3 * tb, KG), F32)],
        compiler_params=pltpu.CompilerParams(
            dimension_semantics=("arbitrary",), vmem_limit_bytes=PEER_VMEM_LIMIT),
        name="peer_out",
    )(tix, w, par, x1_3, g2_3, fw_3, expand, tbl)


def _trunk(x, mod, h0, conv0, c0, n0, m0, chunk_len, p):
    B, T, _ = x.shape
    n = B * T
    tm = min(512, T)
    z, zg = _inproj(x, mod, p["norm1_w"], p["w_main"], p["b_main"], p["w_gate"], p["b_gate"], tm)
    ya, h_last, conv_new = _rglru(z, h0.reshape(B, 1, W_A), conv0, p["conv_w"], p["conv_b"],
                                  p["w_ri"], p["b_ri"], p["lam"], tm)
    m0b = jnp.broadcast_to(m0[:, :, None], (B, H_B, LANES))
    yb, c_new, n_new, m_new = _mlstm(z, zg, c0, n0, m0b, p["mlstm_norm_w"], chunk_len, tm, 1)
    x1, h2 = _outproj(x, ya, yb, mod, p["wo_a"], p["wo_b"], p["norm2_w"], tm)

    tk = 256
    tix_t, par_t, g_t = _retrieve(h2.reshape(n, D_MODEL), p["wq"], p["k1"], p["k2"], tk)
    tix = tix_t.T
    par = par_t.T
    tb = min(64, T)
    expand, sum_even, sum_odd = _pick_patterns()
    w = _peer_act(tix, h2.reshape(n, SUBLANES, LANES), par, g_t.T, sum_even, sum_odd,
                  p["u_tbl"], tb)
    y3 = _peer_out(tix, w, par, x1.reshape(n, SUBLANES, LANES),
                   mod[:, 5].reshape(B, SUBLANES, LANES), p["final_w"], expand, p["v_tbl"], tb, T)
    return (y3.reshape(B, T, D_MODEL), h_last.reshape(B, W_A), conv_new, c_new, n_new,
            m_new[:, :, 0])


def kernel(x_prompt, x_sample, c_prompt, c_sample, state_lru_h, state_lru_conv, state_mlstm_C,
           state_mlstm_n, state_mlstm_m, norm1_w, ada_w, ada_b, w_in, b_in, conv_w, conv_b,
           lru_wr, lru_br, lru_wi, lru_bi, lru_lambda, mlstm_norm_w, w_out, norm2_w,
           peer_wq, peer_k1, peer_k2, peer_u, peer_v, final_norm_w):
    depth = w_in.shape[0]
    assert depth == 1, "single-layer trunk only"
    B = x_prompt.shape[0]
    Bs = x_sample.shape[0]
    l = 0

    def block_diag(w):
        nb, c, d = w.shape
        eye = jnp.eye(nb, dtype=w.dtype)
        return jnp.einsum("ncd,nm->ncmd", w, eye).reshape(nb * c, nb * d)

    pad_g = LANES - N_GATE
    p = {
        "norm1_w": norm1_w[l].reshape(1, D_MODEL),
        "w_main": w_in[l][:, :N_MAIN].astype(BF16),
        "b_main": b_in[l][:N_MAIN].reshape(1, N_MAIN),
        "w_gate": jnp.pad(w_in[l][:, N_MAIN:], ((0, 0), (0, pad_g))),
        "b_gate": jnp.pad(b_in[l][N_MAIN:], (0, pad_g)).reshape(1, LANES),
        "conv_w": conv_w[l],
        "conv_b": conv_b[l].reshape(1, W_A),
        "w_ri": jnp.concatenate([block_diag(lru_wr[l]), block_diag(lru_wi[l])], axis=1).astype(BF16),
        "b_ri": jnp.concatenate([lru_br[l], lru_bi[l]]).reshape(1, 2 * W_A),
        "lam": lru_lambda[l].reshape(1, W_A),
        "mlstm_norm_w": mlstm_norm_w[l].reshape(1, W_B),
        "wo_a": w_out[l][:W_A].astype(BF16),
        "wo_b": w_out[l][W_A:].astype(BF16),
        "norm2_w": norm2_w[l].reshape(1, D_MODEL),
        "wq": peer_wq[l].astype(BF16),
        "k1": peer_k1[l],
        "k2": peer_k2[l],
        "u_tbl": _tile_table(peer_u[l]),
        "v_tbl": _tile_table(peer_v[l]),
        "final_w": final_norm_w.reshape(SUBLANES, LANES),
    }

    mod = _adaln(jnp.concatenate([c_prompt, c_sample], axis=0), ada_w[l], ada_b[l])
    mod = mod.reshape(B + Bs, N_ADA, D_MODEL)

    dt = x_prompt.dtype
    outs_p = _trunk(x_prompt, mod[:B], jnp.zeros((B, W_A), dt), jnp.zeros((B, CONV_W - 1, W_A), dt),
                    jnp.zeros((B, H_B, DK, DK), dt), jnp.zeros((B, H_B, DK), dt),
                    jnp.zeros((B, H_B), dt), CHUNK, p)
    outs_s = _trunk(x_sample, mod[B:], state_lru_h[l], state_lru_conv[l], state_mlstm_C[l],
                    state_mlstm_n[l], state_mlstm_m[l], x_sample.shape[1], p)
    y_p, *st_p = outs_p
    y_s, *st_s = outs_s
    st_p = [s[None].astype(r.dtype) for s, r in zip(
        st_p, (state_lru_h, state_lru_conv, state_mlstm_C, state_mlstm_n, state_mlstm_m))]
    st_s = [s[None].astype(r.dtype) for s, r in zip(
        st_s, (state_lru_h, state_lru_conv, state_mlstm_C, state_mlstm_n, state_mlstm_m))]
    return (y_p, y_s, *st_p, *st_s)
```

```python
import functools

import jax
import jax.numpy as jnp
from jax import lax
from jax.experimental import pallas as pl
from jax.experimental.pallas import tpu as pltpu

D_MODEL = 1024
W_A = 512
W_B = 512
LRU_BLOCKS = 8
CONV_W = 4
LRU_C = 8.0
H_B = 4
DK = W_B // H_B
N_ADA = 6
PEER_HEADS = 8
N_KEYS = 128
D_QUERY = 256
PEER_TOPK = 16
N_PICK = PEER_HEADS * PEER_TOPK
EPS = 1e-6
CHUNK = 64
N_MAIN = 2 * W_A + 4 * W_B
N_GATE = 2 * H_B

LANES = 128
SUBLANES = 8
PEER_VMEM_LIMIT = 48 * 1024 * 1024

F32 = jnp.float32
BF16 = jnp.bfloat16
HIGHEST = lax.Precision.HIGHEST


def _rms(x, w):
    return x * lax.rsqrt(jnp.mean(x * x, axis=-1, keepdims=True) + EPS) * w


def _adaln_kernel(c_ref, w_ref, b_ref, o_ref):
    s = jax.nn.silu(c_ref[...])
    o_ref[...] = jnp.dot(s, w_ref[...], preferred_element_type=F32, precision=HIGHEST) + b_ref[...]


def _adaln(c, w, b):
    nb = c.shape[0]
    n_out = w.shape[1]
    tn = 1024
    return pl.pallas_call(
        _adaln_kernel,
        grid=(n_out // tn,),
        in_specs=[pl.BlockSpec((nb, D_MODEL), lambda j: (0, 0)),
                  pl.BlockSpec((D_MODEL, tn), lambda j: (0, j)),
                  pl.BlockSpec((1, tn), lambda j: (0, j))],
        out_specs=pl.BlockSpec((nb, tn), lambda j: (0, j)),
        out_shape=jax.ShapeDtypeStruct((nb, n_out), F32),
        name="adaln",
    )(c, w, b.reshape(1, n_out))


def _inproj_kernel(x_ref, mod_ref, nw_ref, w_ref, b_ref, wg_ref, bg_ref, z_ref, zg_ref):
    h = _rms(x_ref[0], nw_ref[...]) * (1.0 + mod_ref[0, 1:2, :]) + mod_ref[0, 0:1, :]
    z_ref[0] = jnp.dot(h.astype(BF16), w_ref[...], preferred_element_type=F32) + b_ref[...]
    zg_ref[0] = jnp.dot(h, wg_ref[...], preferred_element_type=F32, precision=HIGHEST) + bg_ref[...]


def _inproj(x, mod, nw, w_main, b_main, w_gate, b_gate, tm):
    B, T, _ = x.shape
    return pl.pallas_call(
        _inproj_kernel,
        grid=(B, T // tm),
        in_specs=[pl.BlockSpec((1, tm, D_MODEL), lambda b, t: (b, t, 0)),
                  pl.BlockSpec((1, N_ADA, D_MODEL), lambda b, t: (b, 0, 0)),
                  pl.BlockSpec((1, D_MODEL), lambda b, t: (0, 0)),
                  pl.BlockSpec((D_MODEL, N_MAIN), lambda b, t: (0, 0)),
                  pl.BlockSpec((1, N_MAIN), lambda b, t: (0, 0)),
                  pl.BlockSpec((D_MODEL, LANES), lambda b, t: (0, 0)),
                  pl.BlockSpec((1, LANES), lambda b, t: (0, 0))],
        out_specs=[pl.BlockSpec((1, tm, N_MAIN), lambda b, t: (b, t, 0)),
                   pl.BlockSpec((1, tm, LANES), lambda b, t: (b, t, 0))],
        out_shape=[jax.ShapeDtypeStruct((B, T, N_MAIN), F32),
                   jax.ShapeDtypeStruct((B, T, LANES), F32)],
        compiler_params=pltpu.CompilerParams(
            dimension_semantics=("arbitrary", "arbitrary"), vmem_limit_bytes=48 * 1024 * 1024),
        name="inproj",
    )(x, mod, nw, w_main, b_main, w_gate, b_gate)


def _rglru_kernel(z_ref, h0_ref, conv0_ref, cw_ref, cb_ref, wri_ref, bri_ref, lam_ref,
                  ya_ref, hlast_ref, convnew_ref, xbuf, hcar):
    t = pl.program_id(1)
    tt = z_ref.shape[1]

    @pl.when(t == 0)
    def _():
        xbuf[0:SUBLANES, :] = jnp.zeros((SUBLANES, W_A), F32)
        xbuf[SUBLANES - (CONV_W - 1):SUBLANES, :] = conv0_ref[0]
        hcar[...] = h0_ref[0]

    xbuf[SUBLANES:SUBLANES + tt, :] = z_ref[0, :, 0:W_A]
    u = cb_ref[...]
    for k in range(CONV_W):
        off = SUBLANES - (CONV_W - 1) + k
        u = u + xbuf[off:off + tt, :] * cw_ref[k:k + 1, :]
    convnew_ref[0] = xbuf[tt + SUBLANES - (CONV_W - 1):tt + SUBLANES, :]
    xbuf[0:SUBLANES, :] = xbuf[tt:tt + SUBLANES, :]

    ri = jnp.dot(u.astype(BF16), wri_ref[...], preferred_element_type=F32) + bri_ref[...]
    r = jax.nn.sigmoid(ri[:, 0:W_A])
    i = jax.nn.sigmoid(ri[:, W_A:2 * W_A])
    log_a = (-LRU_C) * r * jax.nn.softplus(-lam_ref[...])
    a = jnp.exp(log_a)
    bx = jnp.sqrt(jnp.tanh(-log_a) * (a * a + 1.0)) * (i * u)

    row = lax.broadcasted_iota(jnp.int32, (tt, W_A), 0)
    d = 1
    while d < tt:
        keep = row >= d
        a_sh = jnp.where(keep, pltpu.roll(a, d, 0), 1.0)
        b_sh = jnp.where(keep, pltpu.roll(bx, d, 0), 0.0)
        bx = bx + a * b_sh
        a = a * a_sh
        d *= 2
    h = bx + a * hcar[...]
    hcar[...] = h[tt - 1:tt, :]
    hlast_ref[0] = h[tt - 1:tt, :]
    ya_ref[0] = (h * jax.nn.gelu(z_ref[0, :, W_A:2 * W_A])).astype(ya_ref.dtype)


def _rglru(z, h0, conv0, conv_w, conv_b, w_ri, b_ri, lam, tt):
    B, T, _ = z.shape
    return pl.pallas_call(
        _rglru_kernel,
        grid=(B, T // tt),
        in_specs=[pl.BlockSpec((1, tt, 2 * W_A), lambda b, t: (b, t, 0)),
                  pl.BlockSpec((1, 1, W_A), lambda b, t: (b, 0, 0)),
                  pl.BlockSpec((1, CONV_W - 1, W_A), lambda b, t: (b, 0, 0)),
                  pl.BlockSpec((CONV_W, W_A), lambda b, t: (0, 0)),
                  pl.BlockSpec((1, W_A), lambda b, t: (0, 0)),
                  pl.BlockSpec((W_A, 2 * W_A), lambda b, t: (0, 0)),
                  pl.BlockSpec((1, 2 * W_A), lambda b, t: (0, 0)),
                  pl.BlockSpec((1, W_A), lambda b, t: (0, 0))],
        out_specs=[pl.BlockSpec((1, tt, W_A), lambda b, t: (b, t, 0)),
                   pl.BlockSpec((1, 1, W_A), lambda b, t: (b, 0, 0)),
                   pl.BlockSpec((1, CONV_W - 1, W_A), lambda b, t: (b, 0, 0))],
        out_shape=[jax.ShapeDtypeStruct((B, T, W_A), BF16),
                   jax.ShapeDtypeStruct((B, 1, W_A), F32),
                   jax.ShapeDtypeStruct((B, CONV_W - 1, W_A), F32)],
        scratch_shapes=[pltpu.VMEM((tt + SUBLANES, W_A), F32), pltpu.VMEM((1, W_A), F32)],
        compiler_params=pltpu.CompilerParams(dimension_semantics=("arbitrary", "arbitrary")),
        name="rglru",
    )(z, h0, conv0, conv_w, conv_b, w_ri, b_ri, lam)


def _mlstm_kernel(L, zqk_ref, zvo_ref, zg_ref, c0_ref, n0_ref, m0_ref, nw_ref,
                  yb_ref, cout_ref, nout_ref, mout_ref, c_s, n_s, m_s):
    t = pl.program_id(1)
    nb, tt, _ = zqk_ref.shape

    @pl.when(t == 0)
    def _():
        c_s[...] = c0_ref[...]
        n_s[...] = n0_ref[...]
        m_s[...] = m0_ref[...]

    ri = lax.broadcasted_iota(jnp.int32, (L, L), 0)
    ci = lax.broadcasted_iota(jnp.int32, (L, L), 1)
    causal = ci <= ri
    tri = causal.astype(F32)
    eye = (ci == ri).astype(F32)

    def to_row(col):
        return jnp.sum(col * eye, axis=0, keepdims=True)

    def chunk(c, carry):
        r0 = pl.multiple_of(c * L, L)
        rows = pl.ds(r0, L)
        for bi, hd in [(bi, hd) for bi in range(nb) for hd in range(H_B)]:
            g = zg_ref[bi, rows, :]
            cols = slice(hd * DK, (hd + 1) * DK)
            q = zqk_ref[bi, rows, hd * DK:(hd + 1) * DK] * (DK ** -0.5)
            k = zqk_ref[bi, rows, W_B + hd * DK:W_B + (hd + 1) * DK]
            v = zvo_ref[bi, rows, hd * DK:(hd + 1) * DK]
            o = zvo_ref[bi, rows, W_B + hd * DK:W_B + (hd + 1) * DK]
            ig_c = g[:, hd:hd + 1]
            lf_c = jax.nn.log_sigmoid(g[:, H_B + hd:H_B + hd + 1])
            ig_r = to_row(ig_c)
            b_c = jnp.sum(tri * to_row(lf_c), axis=1, keepdims=True)
            b_r = to_row(b_c)
            m_prev = m_s[bi, hd:hd + 1, 0:1]
            n_prev = n_s[bi, hd:hd + 1, :]
            c_prev = c_s[bi, hd]

            log_d = jnp.where(causal, b_c - b_r + ig_r, -jnp.inf)
            inter = b_c + m_prev
            m_t = jnp.maximum(inter, jnp.max(log_d, axis=1, keepdims=True))
            dmat = jnp.exp(log_d - m_t)
            w_state = jnp.exp(inter - m_t)
            qb = q.astype(BF16)
            s = lax.dot_general(qb, k.astype(BF16), (((1,), (1,)), ((), ())),
                                preferred_element_type=F32) * dmat
            num = (jnp.dot(s.astype(BF16), v.astype(BF16), preferred_element_type=F32)
                   + w_state * jnp.dot(qb, c_prev.astype(BF16), preferred_element_type=F32))
            den = (jnp.sum(s, axis=1, keepdims=True)
                   + w_state * jnp.sum(q * n_prev, axis=1, keepdims=True))
            h = num / jnp.maximum(jnp.abs(den), jnp.exp(-m_t))

            m_new = m_t[L - 1:L, :]
            w_k = jnp.exp(b_c[L - 1:L, :] - b_c + ig_c - m_new)
            w_c = jnp.exp(inter[L - 1:L, :] - m_new)
            kw = k * w_k
            c_s[bi, hd] = w_c * c_prev + jnp.dot(kw.T.astype(BF16), v.astype(BF16),
                                                 preferred_element_type=F32)
            n_s[bi, hd:hd + 1, :] = w_c * n_prev + jnp.sum(kw, axis=0, keepdims=True)
            m_s[bi, hd:hd + 1, :] = jnp.broadcast_to(m_new, (1, LANES))

            hn = h * lax.rsqrt(jnp.mean(h * h, axis=1, keepdims=True) + EPS)
            y = jax.nn.sigmoid(o) * (hn * nw_ref[:, cols])
            yb_ref[bi, rows, hd * DK:(hd + 1) * DK] = y.astype(yb_ref.dtype)
        return carry

    lax.fori_loop(0, tt // L, chunk, 0)
    cout_ref[...] = c_s[...]
    nout_ref[...] = n_s[...]
    mout_ref[...] = m_s[...]


def _mlstm(z, zg, c0, n0, m0, norm_w, L, tt, nb):
    B, T, _ = z.shape
    state_specs = [pl.BlockSpec((nb, H_B, DK, DK), lambda b, t: (b, 0, 0, 0)),
                   pl.BlockSpec((nb, H_B, DK), lambda b, t: (b, 0, 0)),
                   pl.BlockSpec((nb, H_B, LANES), lambda b, t: (b, 0, 0))]
    return pl.pallas_call(
        functools.partial(_mlstm_kernel, L),
        grid=(B // nb, T // tt),
        in_specs=[pl.BlockSpec((nb, tt, 2 * W_B), lambda b, t: (b, t, 1)),
                  pl.BlockSpec((nb, tt, 2 * W_B), lambda b, t: (b, t, 2)),
                  pl.BlockSpec((nb, tt, LANES), lambda b, t: (b, t, 0))]
        + state_specs + [pl.BlockSpec((1, W_B), lambda b, t: (0, 0))],
        out_specs=[pl.BlockSpec((nb, tt, W_B), lambda b, t: (b, t, 0))] + state_specs,
        out_shape=[jax.ShapeDtypeStruct((B, T, W_B), BF16),
                   jax.ShapeDtypeStruct((B, H_B, DK, DK), F32),
                   jax.ShapeDtypeStruct((B, H_B, DK), F32),
                   jax.ShapeDtypeStruct((B, H_B, LANES), F32)],
        scratch_shapes=[pltpu.VMEM((nb, H_B, DK, DK), F32), pltpu.VMEM((nb, H_B, DK), F32),
                        pltpu.VMEM((nb, H_B, LANES), F32)],
        compiler_params=pltpu.CompilerParams(dimension_semantics=("arbitrary", "arbitrary")),
        name="mlstm",
    )(z, z, zg, c0, n0, m0, norm_w)


def _outproj_kernel(x_ref, ya_ref, yb_ref, mod_ref, woa_ref, wob_ref, nw_ref, x1_ref, h2_ref):
    mix = (jnp.dot(ya_ref[0], woa_ref[...], preferred_element_type=F32)
           + jnp.dot(yb_ref[0], wob_ref[...], preferred_element_type=F32))
    x1 = x_ref[0] + mod_ref[0, 2:3, :] * mix
    x1_ref[0] = x1
    h2_ref[0] = _rms(x1, nw_ref[...]) * (1.0 + mod_ref[0, 4:5, :]) + mod_ref[0, 3:4, :]


def _outproj(x, ya, yb, mod, wo_a, wo_b, nw, tm):
    B, T, _ = x.shape
    xspec = pl.BlockSpec((1, tm, D_MODEL), lambda b, t: (b, t, 0))
    yspec = pl.BlockSpec((1, tm, W_A), lambda b, t: (b, t, 0))
    return pl.pallas_call(
        _outproj_kernel,
        grid=(B, T // tm),
        in_specs=[xspec, yspec, yspec,
                  pl.BlockSpec((1, N_ADA, D_MODEL), lambda b, t: (b, 0, 0)),
                  pl.BlockSpec((W_A, D_MODEL), lambda b, t: (0, 0)),
                  pl.BlockSpec((W_B, D_MODEL), lambda b, t: (0, 0)),
                  pl.BlockSpec((1, D_MODEL), lambda b, t: (0, 0))],
        out_specs=[xspec, xspec],
        out_shape=[jax.ShapeDtypeStruct((B, T, D_MODEL), F32)] * 2,
        compiler_params=pltpu.CompilerParams(dimension_semantics=("arbitrary", "arbitrary")),
        name="outproj",
    )(x, ya, yb, mod, wo_a, wo_b, nw)


def _topk_rows(s, ids, k):
    ids = ids.astype(F32)
    big = jnp.float32(2 ** 24)
    slot = lax.broadcasted_iota(jnp.int32, (k, s.shape[1]), 0)
    vals = jnp.zeros((k, s.shape[1]), F32)
    idxs = jnp.zeros((k, s.shape[1]), F32)
    for j in range(k):
        m = jnp.max(s, axis=0, keepdims=True)
        i = jnp.min(jnp.where(s == m, ids, big), axis=0, keepdims=True)
        s = jnp.where(ids == i, -jnp.inf, s)
        vals = jnp.where(slot == j, m, vals)
        idxs = jnp.where(slot == j, i, idxs)
    return vals, idxs.astype(jnp.int32)


def _pair_candidates(v1, v2):
    k, tk = v1.shape
    row = lax.broadcasted_iota(jnp.int32, (SUBLANES, tk), 0)
    sums = [v1[0:1, :] + v2]
    ids = [lax.broadcasted_iota(jnp.int32, (k, tk), 0)]
    for i in range(1, SUBLANES):
        blk = v1[i:i + 1, :] + v2[0:SUBLANES, :]
        sums.append(jnp.where(row < k // (i + 1), blk, -jnp.inf))
        ids.append(row + i * k)
    for i0 in range(SUBLANES, k, SUBLANES):
        sums.append(v1[i0:i0 + SUBLANES, :] + v2[0:1, :])
        ids.append((row + i0) * k)
    return jnp.concatenate(sums, axis=0), jnp.concatenate(ids, axis=0)


def _select_rows(tbl, sel):
    out = jnp.zeros_like(tbl)
    for i in range(tbl.shape[0]):
        out = jnp.where(sel == i, tbl[i:i + 1, :], out)
    return out


def _retrieve_kernel(h2_ref, wq_ref, k1_ref, k2_ref, tix_ref, par_ref, g_ref):
    q = jnp.dot(h2_ref[...].astype(BF16), wq_ref[...], preferred_element_type=F32)
    half = D_QUERY // 2
    nt = (((1,), (1,)), ((), ()))
    key_ids = lax.broadcasted_iota(jnp.int32, (N_KEYS, q.shape[0]), 0).astype(F32)
    for hd in range(PEER_HEADS):
        qh = q[:, hd * D_QUERY:(hd + 1) * D_QUERY]
        qh = qh * lax.rsqrt(jnp.mean(qh * qh, axis=1, keepdims=True) + EPS)
        s1 = lax.dot_general(k1_ref[...], qh[:, :half], nt, preferred_element_type=F32,
                             precision=HIGHEST)
        s2 = lax.dot_general(k2_ref[...], qh[:, half:], nt, preferred_element_type=F32,
                             precision=HIGHEST)
        v1, i1 = _topk_rows(s1, key_ids, PEER_TOPK)
        v2, i2 = _topk_rows(s2, key_ids, PEER_TOPK)
        cand, cand_ids = _pair_candidates(v1, v2)
        sv, si = _topk_rows(cand, cand_ids, PEER_TOPK)
        e1 = _select_rows(i1, lax.shift_right_logical(si, 4))
        e2 = _select_rows(i2, si & (PEER_TOPK - 1))
        expert = e1 * N_KEYS + e2
        ex = jnp.exp(sv - jnp.max(sv, axis=0, keepdims=True))
        rows = slice(hd * PEER_TOPK, (hd + 1) * PEER_TOPK)
        tix_ref[rows, :] = lax.shift_right_logical(expert, 1) * SUBLANES
        par_ref[rows, :] = expert & 1
        g_ref[rows, :] = ex / jnp.sum(ex, axis=0, keepdims=True)


def _retrieve(h2, wq, k1, k2, tk):
    n = h2.shape[0]
    ospec = pl.BlockSpec((N_PICK, tk), lambda i: (0, i))
    return pl.pallas_call(
        _retrieve_kernel,
        grid=(n // tk,),
        in_specs=[pl.BlockSpec((tk, D_MODEL), lambda i: (i, 0)),
                  pl.BlockSpec((D_MODEL, PEER_HEADS * D_QUERY), lambda i: (0, 0)),
                  pl.BlockSpec((N_KEYS, D_QUERY // 2), lambda i: (0, 0)),
                  pl.BlockSpec((N_KEYS, D_QUERY // 2), lambda i: (0, 0))],
        out_specs=[ospec, ospec, ospec],
        out_shape=[jax.ShapeDtypeStruct((N_PICK, n), jnp.int32),
                   jax.ShapeDtypeStruct((N_PICK, n), jnp.int32),
                   jax.ShapeDtypeStruct((N_PICK, n), F32)],
        compiler_params=pltpu.CompilerParams(dimension_semantics=("arbitrary",)),
        name="retrieve",
    )(h2, wq, k1, k2)


TILE_ROWS = 2 * SUBLANES
TOKENS_PER_STEP = 16
KG = N_PICK * TILE_ROWS


def _tile_table(tbl):
    halves = lax.bitcast_convert_type(tbl.astype(BF16), jnp.uint16).astype(jnp.uint32)
    halves = halves.reshape(-1, SUBLANES, 2, LANES)
    words = halves[:, :, 0, :] | (halves[:, :, 1, :] << 16)
    return words.reshape(-1, LANES)


def _gathered_rows(tbl_ref, off_ref, p):
    tiles = []
    for j in range(N_PICK):
        off = pl.multiple_of(off_ref[p, j], SUBLANES)
        tiles.append(pltpu.bitcast(tbl_ref[pl.ds(off, SUBLANES), :], BF16))
    return jnp.concatenate(tiles, axis=0)


def _split_bf16(x):
    hi = x.astype(BF16)
    return hi, (x - hi.astype(F32)).astype(BF16)


def _pick_patterns():
    col = jnp.arange(KG, dtype=jnp.int32)
    pick = jnp.arange(N_PICK, dtype=jnp.int32)
    expand = (col[None, :] // TILE_ROWS == pick[:, None]).astype(BF16)
    half = col[:, None] // SUBLANES
    sum_even = (half == 2 * pick[None, :]).astype(BF16)
    sum_odd = (half == 2 * pick[None, :] + 1).astype(BF16)
    return expand, sum_even, sum_odd


def _peer_act_kernel(tix_ref, x_ref, par_ref, g_ref, sum_even_ref, sum_odd_ref, tbl_ref, w_ref,
                     xa_scr, zs_scr):
    tb = x_ref.shape[0]
    x = x_ref[...]
    xa_scr[...] = jnp.concatenate([x, x - x.astype(BF16).astype(F32)], axis=1).astype(BF16)
    keep = ((lax.broadcasted_iota(jnp.int32, (SUBLANES, KG), 1) & (SUBLANES - 1))
            == lax.broadcasted_iota(jnp.int32, (SUBLANES, KG), 0))
    nt = (((1,), (1,)), ((), ()))

    def one_token(p):
        rows = _gathered_rows(tbl_ref, tix_ref, p)
        res = lax.dot_general(xa_scr[p], rows, nt, preferred_element_type=F32)
        z = jnp.where(keep, res[0:SUBLANES] + res[SUBLANES:], 0.0)
        zs_scr[pl.ds(p, 1), :] = jnp.sum(z, axis=0, keepdims=True)

    def group(i, carry):
        for k in range(TOKENS_PER_STEP):
            one_token(TOKENS_PER_STEP * i + k)
        return carry

    lax.fori_loop(0, tb // TOKENS_PER_STEP, group, 0)

    sum_even = sum_even_ref[...]
    sum_odd = sum_odd_ref[...]
    z_hi, z_lo = _split_bf16(zs_scr[...])
    even = (jnp.dot(z_hi, sum_even, preferred_element_type=F32)
            + jnp.dot(z_lo, sum_even, preferred_element_type=F32))
    odd = (jnp.dot(z_hi, sum_odd, preferred_element_type=F32)
           + jnp.dot(z_lo, sum_odd, preferred_element_type=F32))
    act = jnp.where(par_ref[...] == 1, odd, even)
    w_ref[...] = g_ref[...] * jax.nn.gelu(act)


def _peer_act(tix, x3, par, gates, sum_even, sum_odd, tbl, tb):
    n = tix.shape[0]
    vspec = pl.BlockSpec((tb, N_PICK), lambda i: (i, 0))
    whole = pl.BlockSpec(memory_space=pltpu.VMEM)
    return pl.pallas_call(
        _peer_act_kernel,
        grid=(n // tb,),
        in_specs=[pl.BlockSpec((tb, N_PICK), lambda i: (i, 0), memory_space=pltpu.SMEM),
                  pl.BlockSpec((tb, SUBLANES, LANES), lambda i: (i, 0, 0)),
                  vspec, vspec, whole, whole, whole],
        out_specs=vspec,
        out_shape=jax.ShapeDtypeStruct((n, N_PICK), F32),
        scratch_shapes=[pltpu.VMEM((tb, TILE_ROWS, LANES), BF16), pltpu.VMEM((tb, KG), F32)],
        compiler_params=pltpu.CompilerParams(
            dimension_semantics=("arbitrary",), vmem_limit_bytes=PEER_VMEM_LIMIT),
        name="peer_act",
    )(tix, x3, par, gates, sum_even, sum_odd, tbl)


def _peer_out_kernel(tix_ref, w_ref, par_ref, x1_ref, g2_ref, fw_ref, expand_ref, tbl_ref, y_ref,
                     wexp_scr):
    tb = x1_ref.shape[0]

    expand = expand_ref[...]
    w_hi, w_lo = _split_bf16(w_ref[...])
    par8 = (par_ref[...] * SUBLANES).astype(F32).astype(BF16)
    wexp_scr[0:tb, :] = jnp.dot(w_hi, expand, preferred_element_type=F32)
    wexp_scr[tb:2 * tb, :] = jnp.dot(w_lo, expand, preferred_element_type=F32)
    wexp_scr[2 * tb:3 * tb, :] = jnp.dot(par8, expand, preferred_element_type=F32)

    rel = ((lax.broadcasted_iota(jnp.int32, (SUBLANES, KG), 1) & (TILE_ROWS - 1))
           - lax.broadcasted_iota(jnp.int32, (SUBLANES, KG), 0)).astype(F32)

    def one_token(p):
        sel = rel == wexp_scr[pl.ds(2 * tb + p, 1), :]
        a_hi = jnp.where(sel, wexp_scr[pl.ds(p, 1), :], 0.0)
        a_lo = jnp.where(sel, wexp_scr[pl.ds(tb + p, 1), :], 0.0)
        a = jnp.concatenate([a_hi, a_lo], axis=0).astype(BF16)
        rows = _gathered_rows(tbl_ref, tix_ref, p)
        res = jnp.dot(a, rows, preferred_element_type=F32)
        y_ref[p] = res[0:SUBLANES] + res[SUBLANES:]

    def group(i, carry):
        for k in range(TOKENS_PER_STEP):
            one_token(TOKENS_PER_STEP * i + k)
        return carry

    lax.fori_loop(0, tb // TOKENS_PER_STEP, group, 0)

    x2 = x1_ref[...] + g2_ref[...] * y_ref[...]
    ms = jnp.sum(jnp.sum(x2 * x2, axis=2, keepdims=True), axis=1, keepdims=True) / D_MODEL
    y_ref[...] = x2 * lax.rsqrt(ms + EPS) * fw_ref[...]


def _peer_out(tix, w, par, x1_3, g2_3, fw_3, expand, tbl, tb, tokens_per_batch):
    n = tix.shape[0]
    vspec = pl.BlockSpec((tb, N_PICK), lambda i: (i, 0))
    xspec = pl.BlockSpec((tb, SUBLANES, LANES), lambda i: (i, 0, 0))
    whole = pl.BlockSpec(memory_space=pltpu.VMEM)
    return pl.pallas_call(
        _peer_out_kernel,
        grid=(n // tb,),
        in_specs=[pl.BlockSpec((tb, N_PICK), lambda i: (i, 0), memory_space=pltpu.SMEM),
                  vspec, vspec, xspec,
                  pl.BlockSpec((1, SUBLANES, LANES), lambda i: ((i * tb) // tokens_per_batch, 0, 0)),
                  pl.BlockSpec((SUBLANES, LANES), lambda i: (0, 0)),
                  whole, whole],
        out_specs=xspec,
        out_shape=jax.ShapeDtypeStruct((n, SUBLANES, LANES), F32),
        scratch_shapes=[pltpu.VMEM((3 * tb, KG), F32)],
        compiler_params=pltpu.CompilerParams(
            dimension_semantics=("arbitrary",), vmem_limit_bytes=PEER_VMEM_LIMIT),
        name="peer_out",
    )(tix, w, par, x1_3, g2_3, fw_3, expand, tbl)


def _trunk(x, mod, h0, conv0, c0, n0, m0, chunk_len, p):
    B, T, _ = x.shape
    n = B * T
    tm = min(512, T)
    z, zg = _inproj(x, mod, p["norm1_w"], p["w_main"], p["b_main"], p["w_gate"], p["b_gate"], tm)
    ya, h_last, conv_new = _rglru(z, h0.reshape(B, 1, W_A), conv0, p["conv_w"], p["conv_b"],
                                  p["w_ri"], p["b_ri"], p["lam"], tm)
    m0b = jnp.broadcast_to(m0[:, :, None], (B, H_B, LANES))
    yb, c_new, n_new, m_new = _mlstm(z, zg, c0, n0, m0b, p["mlstm_norm_w"], chunk_len, tm, 1)
    x1, h2 = _outproj(x, ya, yb, mod, p["wo_a"], p["wo_b"], p["norm2_w"], tm)

    tk = 256
    tix_t, par_t, g_t = _retrieve(h2.reshape(n, D_MODEL), p["wq"], p["k1"], p["k2"], tk)
    tix = tix_t.T
    par = par_t.T
    tb = min(64, T)
    expand, sum_even, sum_odd = _pick_patterns()
    w = _peer_act(tix, h2.reshape(n, SUBLANES, LANES), par, g_t.T, sum_even, sum_odd,
                  p["u_tbl"], tb)
    y3 = _peer_out(tix, w, par, x1.reshape(n, SUBLANES, LANES),
                   mod[:, 5].reshape(B, SUBLANES, LANES), p["final_w"], expand, p["v_tbl"], tb, T)
    return (y3.reshape(B, T, D_MODEL), h_last.reshape(B, W_A), conv_new, c_new, n_new,
            m_new[:, :, 0])


def kernel(x_prompt, x_sample, c_prompt, c_sample, state_lru_h, state_lru_conv, state_mlstm_C,
           state_mlstm_n, state_mlstm_m, norm1_w, ada_w, ada_b, w_in, b_in, conv_w, conv_b,
           lru_wr, lru_br, lru_wi, lru_bi, lru_lambda, mlstm_norm_w, w_out, norm2_w,
           peer_wq, peer_k1, peer_k2, peer_u, peer_v, final_norm_w):
    depth = w_in.shape[0]
    assert depth == 1, "single-layer trunk only"
    B = x_prompt.shape[0]
    Bs = x_sample.shape[0]
    l = 0

    def block_diag(w):
        nb, c, d = w.shape
        eye = jnp.eye(nb, dtype=w.dtype)
        return jnp.einsum("ncd,nm->ncmd", w, eye).reshape(nb * c, nb * d)

    pad_g = LANES - N_GATE
    p = {
        "norm1_w": norm1_w[l].reshape(1, D_MODEL),
        "w_main": w_in[l][:, :N_MAIN].astype(BF16),
        "b_main": b_in[l][:N_MAIN].reshape(1, N_MAIN),
        "w_gate": jnp.pad(w_in[l][:, N_MAIN:], ((0, 0), (0, pad_g))),
        "b_gate": jnp.pad(b_in[l][N_MAIN:], (0, pad_g)).reshape(1, LANES),
        "conv_w": conv_w[l],
        "conv_b": conv_b[l].reshape(1, W_A),
        "w_ri": jnp.concatenate([block_diag(lru_wr[l]), block_diag(lru_wi[l])], axis=1).astype(BF16),
        "b_ri": jnp.concatenate([lru_br[l], lru_bi[l]]).reshape(1, 2 * W_A),
        "lam": lru_lambda[l].reshape(1, W_A),
        "mlstm_norm_w": mlstm_norm_w[l].reshape(1, W_B),
        "wo_a": w_out[l][:W_A].astype(BF16),
        "wo_b": w_out[l][W_A:].astype(BF16),
        "norm2_w": norm2_w[l].reshape(1, D_MODEL),
        "wq": peer_wq[l].astype(BF16),
        "k1": peer_k1[l],
        "k2": peer_k2[l],
        "u_tbl": _tile_table(peer_u[l]),
        "v_tbl": _tile_table(peer_v[l]),
        "final_w": final_norm_w.reshape(SUBLANES, LANES),
    }

    mod = _adaln(jnp.concatenate([c_prompt, c_sample], axis=0), ada_w[l], ada_b[l])
    mod = mod.reshape(B + Bs, N_ADA, D_MODEL)

    dt = x_prompt.dtype
    outs_p = _trunk(x_prompt, mod[:B], jnp.zeros((B, W_A), dt), jnp.zeros((B, CONV_W - 1, W_A), dt),
                    jnp.zeros((B, H_B, DK, DK), dt), jnp.zeros((B, H_B, DK), dt),
                    jnp.zeros((B, H_B), dt), CHUNK, p)
    outs_s = _trunk(x_sample, mod[B:], state_lru_h[l], state_lru_conv[l], state_mlstm_C[l],
                    state_mlstm_n[l], state_mlstm_m[l], x_sample.shape[1], p)
    y_p, *st_p = outs_p
    y_s, *st_s = outs_s
    st_p = [s[None].astype(r.dtype) for s, r in zip(
        st_p, (state_lru_h, state_lru_conv, state_mlstm_C, state_mlstm_n, state_mlstm_m))]
    st_s = [s[None].astype(r.dtype) for s, r in zip(
        st_s, (state_lru_h, state_lru_conv, state_mlstm_C, state_mlstm_n, state_mlstm_m))]
    return (y_p, y_s, *st_p, *st_s)
```

```python
import functools

import jax
import jax.numpy as jnp
from jax import lax
from jax.experimental import pallas as pl
from jax.experimental.pallas import tpu as pltpu

D_MODEL = 1024
W_A = 512
W_B = 512
LRU_BLOCKS = 8
CONV_W = 4
LRU_C = 8.0
H_B = 4
DK = W_B // H_B
N_ADA = 6
PEER_HEADS = 8
N_KEYS = 128
D_QUERY = 256
PEER_TOPK = 16
N_PICK = PEER_HEADS * PEER_TOPK
EPS = 1e-6
CHUNK = 64
N_MAIN = 2 * W_A + 4 * W_B
N_GATE = 2 * H_B

LANES = 128
SUBLANES = 8
PEER_VMEM_LIMIT = 48 * 1024 * 1024

F32 = jnp.float32
BF16 = jnp.bfloat16
HIGHEST = lax.Precision.HIGHEST


def _rms(x, w):
    return x * lax.rsqrt(jnp.mean(x * x, axis=-1, keepdims=True) + EPS) * w


def _adaln_kernel(c_ref, w_ref, b_ref, o_ref):
    s = jax.nn.silu(c_ref[...])
    o_ref[...] = jnp.dot(s, w_ref[...], preferred_element_type=F32, precision=HIGHEST) + b_ref[...]


def _adaln(c, w, b):
    nb = c.shape[0]
    n_out = w.shape[1]
    tn = 1024
    return pl.pallas_call(
        _adaln_kernel,
        grid=(n_out // tn,),
        in_specs=[pl.BlockSpec((nb, D_MODEL), lambda j: (0, 0)),
                  pl.BlockSpec((D_MODEL, tn), lambda j: (0, j)),
                  pl.BlockSpec((1, tn), lambda j: (0, j))],
        out_specs=pl.BlockSpec((nb, tn), lambda j: (0, j)),
        out_shape=jax.ShapeDtypeStruct((nb, n_out), F32),
        name="adaln",
    )(c, w, b.reshape(1, n_out))


def _inproj_kernel(x_ref, mod_ref, nw_ref, w_ref, b_ref, wg_ref, bg_ref, z_ref, zg_ref):
    h = _rms(x_ref[0], nw_ref[...]) * (1.0 + mod_ref[0, 1:2, :]) + mod_ref[0, 0:1, :]
    z_ref[0] = jnp.dot(h.astype(BF16), w_ref[...], preferred_element_type=F32) + b_ref[...]
    zg_ref[0] = jnp.dot(h, wg_ref[...], preferred_element_type=F32, precision=HIGHEST) + bg_ref[...]


def _inproj(x, mod, nw, w_main, b_main, w_gate, b_gate, tm):
    B, T, _ = x.shape
    return pl.pallas_call(
        _inproj_kernel,
        grid=(B, T // tm),
        in_specs=[pl.BlockSpec((1, tm, D_MODEL), lambda b, t: (b, t, 0)),
                  pl.BlockSpec((1, N_ADA, D_MODEL), lambda b, t: (b, 0, 0)),
                  pl.BlockSpec((1, D_MODEL), lambda b, t: (0, 0)),
                  pl.BlockSpec((D_MODEL, N_MAIN), lambda b, t: (0, 0)),
                  pl.BlockSpec((1, N_MAIN), lambda b, t: (0, 0)),
                  pl.BlockSpec((D_MODEL, LANES), lambda b, t: (0, 0)),
                  pl.BlockSpec((1, LANES), lambda b, t: (0, 0))],
        out_specs=[pl.BlockSpec((1, tm, N_MAIN), lambda b, t: (b, t, 0)),
                   pl.BlockSpec((1, tm, LANES), lambda b, t: (b, t, 0))],
        out_shape=[jax.ShapeDtypeStruct((B, T, N_MAIN), F32),
                   jax.ShapeDtypeStruct((B, T, LANES), F32)],
        compiler_params=pltpu.CompilerParams(
            dimension_semantics=("arbitrary", "arbitrary"), vmem_limit_bytes=48 * 1024 * 1024),
        name="inproj",
    )(x, mod, nw, w_main, b_main, w_gate, b_gate)


def _rglru_kernel(z_ref, h0_ref, conv0_ref, cw_ref, cb_ref, wri_ref, bri_ref, lam_ref,
                  ya_ref, hlast_ref, convnew_ref, xbuf, hcar):
    t = pl.program_id(1)
    tt = z_ref.shape[1]

    @pl.when(t == 0)
    def _():
        xbuf[0:SUBLANES, :] = jnp.zeros((SUBLANES, W_A), F32)
        xbuf[SUBLANES - (CONV_W - 1):SUBLANES, :] = conv0_ref[0]
        hcar[...] = h0_ref[0]

    xbuf[SUBLANES:SUBLANES + tt, :] = z_ref[0, :, 0:W_A]
    u = cb_ref[...]
    for k in range(CONV_W):
        off = SUBLANES - (CONV_W - 1) + k
        u = u + xbuf[off:off + tt, :] * cw_ref[k:k + 1, :]
    convnew_ref[0] = xbuf[tt + SUBLANES - (CONV_W - 1):tt + SUBLANES, :]
    xbuf[0:SUBLANES, :] = xbuf[tt:tt + SUBLANES, :]

    ri = jnp.dot(u.astype(BF16), wri_ref[...], preferred_element_type=F32) + bri_ref[...]
    r = jax.nn.sigmoid(ri[:, 0:W_A])
    i = jax.nn.sigmoid(ri[:, W_A:2 * W_A])
    log_a = (-LRU_C) * r * jax.nn.softplus(-lam_ref[...])
    a = jnp.exp(log_a)
    bx = jnp.sqrt(jnp.tanh(-log_a) * (a * a + 1.0)) * (i * u)

    row = lax.broadcasted_iota(jnp.int32, (tt, W_A), 0)
    d = 1
    while d < tt:
        keep = row >= d
        a_sh = jnp.where(keep, pltpu.roll(a, d, 0), 1.0)
        b_sh = jnp.where(keep, pltpu.roll(bx, d, 0), 0.0)
        bx = bx + a * b_sh
        a = a * a_sh
        d *= 2
    h = bx + a * hcar[...]
    hcar[...] = h[tt - 1:tt, :]
    hlast_ref[0] = h[tt - 1:tt, :]
    ya_ref[0] = (h * jax.nn.gelu(z_ref[0, :, W_A:2 * W_A])).astype(ya_ref.dtype)


def _rglru(z, h0, conv0, conv_w, conv_b, w_ri, b_ri, lam, tt):
    B, T, _ = z.shape
    return pl.pallas_call(
        _rglru_kernel,
        grid=(B, T // tt),
        in_specs=[pl.BlockSpec((1, tt, 2 * W_A), lambda b, t: (b, t, 0)),
                  pl.BlockSpec((1, 1, W_A), lambda b, t: (b, 0, 0)),
                  pl.BlockSpec((1, CONV_W - 1, W_A), lambda b, t: (b, 0, 0)),
                  pl.BlockSpec((CONV_W, W_A), lambda b, t: (0, 0)),
                  pl.BlockSpec((1, W_A), lambda b, t: (0, 0)),
                  pl.BlockSpec((W_A, 2 * W_A), lambda b, t: (0, 0)),
                  pl.BlockSpec((1, 2 * W_A), lambda b, t: (0, 0)),
                  pl.BlockSpec((1, W_A), lambda b, t: (0, 0))],
        out_specs=[pl.BlockSpec((1, tt, W_A), lambda b, t: (b, t, 0)),
                   pl.BlockSpec((1, 1, W_A), lambda b, t: (b, 0, 0)),
                   pl.BlockSpec((1, CONV_W - 1, W_A), lambda b, t: (b, 0, 0))],
        out_shape=[jax.ShapeDtypeStruct((B, T, W_A), BF16),
                   jax.ShapeDtypeStruct((B, 1, W_A), F32),
                   jax.ShapeDtypeStruct((B, CONV_W - 1, W_A), F32)],
        scratch_shapes=[pltpu.VMEM((tt + SUBLANES, W_A), F32), pltpu.VMEM((1, W_A), F32)],
        compiler_params=pltpu.CompilerParams(dimension_semantics=("arbitrary", "arbitrary")),
        name="rglru",
    )(z, h0, conv0, conv_w, conv_b, w_ri, b_ri, lam)


def _mlstm_kernel(L, zqk_ref, zvo_ref, zg_ref, c0_ref, n0_ref, m0_ref, nw_ref,
                  yb_ref, cout_ref, nout_ref, mout_ref, c_s, n_s, m_s):
    t = pl.program_id(1)
    nb, tt, _ = zqk_ref.shape

    @pl.when(t == 0)
    def _():
        c_s[...] = c0_ref[...]
        n_s[...] = n0_ref[...]
        m_s[...] = m0_ref[...]

    ri = lax.broadcasted_iota(jnp.int32, (L, L), 0)
    ci = lax.broadcasted_iota(jnp.int32, (L, L), 1)
    causal = ci <= ri
    tri = causal.astype(F32)
    eye = (ci == ri).astype(F32)

    def to_row(col):
        return jnp.sum(col * eye, axis=0, keepdims=True)

    def chunk(c, carry):
        r0 = pl.multiple_of(c * L, L)
        rows = pl.ds(r0, L)
        for bi, hd in [(bi, hd) for bi in range(nb) for hd in range(H_B)]:
            g = zg_ref[bi, rows, :]
            cols = slice(hd * DK, (hd + 1) * DK)
            q = zqk_ref[bi, rows, hd * DK:(hd + 1) * DK] * (DK ** -0.5)
            k = zqk_ref[bi, rows, W_B + hd * DK:W_B + (hd + 1) * DK]
            v = zvo_ref[bi, rows, hd * DK:(hd + 1) * DK]
            o = zvo_ref[bi, rows, W_B + hd * DK:W_B + (hd + 1) * DK]
            ig_c = g[:, hd:hd + 1]
            lf_c = jax.nn.log_sigmoid(g[:, H_B + hd:H_B + hd + 1])
            ig_r = to_row(ig_c)
            b_c = jnp.sum(tri * to_row(lf_c), axis=1, keepdims=True)
            b_r = to_row(b_c)
            m_prev = m_s[bi, hd:hd + 1, 0:1]
            n_prev = n_s[bi, hd:hd + 1, :]
            c_prev = c_s[bi, hd]

            log_d = jnp.where(causal, b_c - b_r + ig_r, -jnp.inf)
            inter = b_c + m_prev
            m_t = jnp.maximum(inter, jnp.max(log_d, axis=1, keepdims=True))
            dmat = jnp.exp(log_d - m_t)
            w_state = jnp.exp(inter - m_t)
            qb = q.astype(BF16)
            s = lax.dot_general(qb, k.astype(BF16), (((1,), (1,)), ((), ())),
                                preferred_element_type=F32) * dmat
            num = (jnp.dot(s.astype(BF16), v.astype(BF16), preferred_element_type=F32)
                   + w_state * jnp.dot(qb, c_prev.astype(BF16), preferred_element_type=F32))
            den = (jnp.sum(s, axis=1, keepdims=True)
                   + w_state * jnp.sum(q * n_prev, axis=1, keepdims=True))
            h = num / jnp.maximum(jnp.abs(den), jnp.exp(-m_t))

            m_new = m_t[L - 1:L, :]
            w_k = jnp.exp(b_c[L - 1:L, :] - b_c + ig_c - m_new)
            w_c = jnp.exp(inter[L - 1:L, :] - m_new)
            kw = k * w_k
            c_s[bi, hd] = w_c * c_prev + jnp.dot(kw.T.astype(BF16), v.astype(BF16),
                                                 preferred_element_type=F32)
            n_s[bi, hd:hd + 1, :] = w_c * n_prev + jnp.sum(kw, axis=0, keepdims=True)
            m_s[bi, hd:hd + 1, :] = jnp.broadcast_to(m_new, (1, LANES))

            hn = h * lax.rsqrt(jnp.mean(h * h, axis=1, keepdims=True) + EPS)
            y = jax.nn.sigmoid(o) * (hn * nw_ref[:, cols])
            yb_ref[bi, rows, hd * DK:(hd + 1) * DK] = y.astype(yb_ref.dtype)
        return carry

    lax.fori_loop(0, tt // L, chunk, 0)
    cout_ref[...] = c_s[...]
    nout_ref[...] = n_s[...]
    mout_ref[...] = m_s[...]


def _mlstm(z, zg, c0, n0, m0, norm_w, L, tt, nb):
    B, T, _ = z.shape
    state_specs = [pl.BlockSpec((nb, H_B, DK, DK), lambda b, t: (b, 0, 0, 0)),
                   pl.BlockSpec((nb, H_B, DK), lambda b, t: (b, 0, 0)),
                   pl.BlockSpec((nb, H_B, LANES), lambda b, t: (b, 0, 0))]
    return pl.pallas_call(
        functools.partial(_mlstm_kernel, L),
        grid=(B // nb, T // tt),
        in_specs=[pl.BlockSpec((nb, tt, 2 * W_B), lambda b, t: (b, t, 1)),
                  pl.BlockSpec((nb, tt, 2 * W_B), lambda b, t: (b, t, 2)),
                  pl.BlockSpec((nb, tt, LANES), lambda b, t: (b, t, 0))]
        + state_specs + [pl.BlockSpec((1, W_B), lambda b, t: (0, 0))],
        out_specs=[pl.BlockSpec((nb, tt, W_B), lambda b, t: (b, t, 0))] + state_specs,
        out_shape=[jax.ShapeDtypeStruct((B, T, W_B), BF16),
                   jax.ShapeDtypeStruct((B, H_B, DK, DK), F32),
                   jax.ShapeDtypeStruct((B, H_B, DK), F32),
                   jax.ShapeDtypeStruct((B, H_B, LANES), F32)],
        scratch_shapes=[pltpu.VMEM((nb, H_B, DK, DK), F32), pltpu.VMEM((nb, H_B, DK), F32),
                        pltpu.VMEM((nb, H_B, LANES), F32)],
        compiler_params=pltpu.CompilerParams(dimension_semantics=("arbitrary", "arbitrary")),
        name="mlstm",
    )(z, z, zg, c0, n0, m0, norm_w)


def _outproj_kernel(x_ref, ya_ref, yb_ref, mod_ref, woa_ref, wob_ref, nw_ref, x1_ref, h2_ref):
    mix = (jnp.dot(ya_ref[0], woa_ref[...], preferred_element_type=F32)
           + jnp.dot(yb_ref[0], wob_ref[...], preferred_element_type=F32))
    x1 = x_ref[0] + mod_ref[0, 2:3, :] * mix
    x1_ref[0] = x1
    h2_ref[0] = _rms(x1, nw_ref[...]) * (1.0 + mod_ref[0, 4:5, :]) + mod_ref[0, 3:4, :]


def _outproj(x, ya, yb, mod, wo_a, wo_b, nw, tm):
    B, T, _ = x.shape
    xspec = pl.BlockSpec((1, tm, D_MODEL), lambda b, t: (b, t, 0))
    yspec = pl.BlockSpec((1, tm, W_A), lambda b, t: (b, t, 0))
    return pl.pallas_call(
        _outproj_kernel,
        grid=(B, T // tm),
        in_specs=[xspec, yspec, yspec,
                  pl.BlockSpec((1, N_ADA, D_MODEL), lambda b, t: (b, 0, 0)),
                  pl.BlockSpec((W_A, D_MODEL), lambda b, t: (0, 0)),
                  pl.BlockSpec((W_B, D_MODEL), lambda b, t: (0, 0)),
                  pl.BlockSpec((1, D_MODEL), lambda b, t: (0, 0))],
        out_specs=[xspec, xspec],
        out_shape=[jax.ShapeDtypeStruct((B, T, D_MODEL), F32)] * 2,
        compiler_params=pltpu.CompilerParams(dimension_semantics=("arbitrary", "arbitrary")),
        name="outproj",
    )(x, ya, yb, mod, wo_a, wo_b, nw)


def _topk_rows(s, ids, k):
    ids = ids.astype(F32)
    big = jnp.float32(2 ** 24)
    slot = lax.broadcasted_iota(jnp.int32, (k, s.shape[1]), 0)
    vals = jnp.zeros((k, s.shape[1]), F32)
    idxs = jnp.zeros((k, s.shape[1]), F32)
    for j in range(k):
        m = jnp.max(s, axis=0, keepdims=True)
        i = jnp.min(jnp.where(s == m, ids, big), axis=0, keepdims=True)
        s = jnp.where(ids == i, -jnp.inf, s)
        vals = jnp.where(slot == j, m, vals)
        idxs = jnp.where(slot == j, i, idxs)
    return vals, idxs.astype(jnp.int32)


def _pair_candidates(v1, v2):
    k, tk = v1.shape
    row = lax.broadcasted_iota(jnp.int32, (SUBLANES, tk), 0)
    sums = [v1[0:1, :] + v2]
    ids = [lax.broadcasted_iota(jnp.int32, (k, tk), 0)]
    for i in range(1, SUBLANES):
        blk = v1[i:i + 1, :] + v2[0:SUBLANES, :]
        sums.append(jnp.where(row < k // (i + 1), blk, -jnp.inf))
        ids.append(row + i * k)
    for i0 in range(SUBLANES, k, SUBLANES):
        sums.append(v1[i0:i0 + SUBLANES, :] + v2[0:1, :])
        ids.append((row + i0) * k)
    return jnp.concatenate(sums, axis=0), jnp.concatenate(ids, axis=0)


def _select_rows(tbl, sel):
    out = jnp.zeros_like(tbl)
    for i in range(tbl.shape[0]):
        out = jnp.where(sel == i, tbl[i:i + 1, :], out)
    return out


def _retrieve_kernel(h2_ref, wq_ref, k1_ref, k2_ref, tix_ref, par_ref, g_ref):
    q = jnp.dot(h2_ref[...].astype(BF16), wq_ref[...], preferred_element_type=F32)
    half = D_QUERY // 2
    nt = (((1,), (1,)), ((), ()))
    key_ids = lax.broadcasted_iota(jnp.int32, (N_KEYS, q.shape[0]), 0).astype(F32)
    for hd in range(PEER_HEADS):
        qh = q[:, hd * D_QUERY:(hd + 1) * D_QUERY]
        qh = qh * lax.rsqrt(jnp.mean(qh * qh, axis=1, keepdims=True) + EPS)
        s1 = lax.dot_general(k1_ref[...], qh[:, :half], nt, preferred_element_type=F32,
                             precision=HIGHEST)
        s2 = lax.dot_general(k2_ref[...], qh[:, half:], nt, preferred_element_type=F32,
                             precision=HIGHEST)
        v1, i1 = _topk_rows(s1, key_ids, PEER_TOPK)
        v2, i2 = _topk_rows(s2, key_ids, PEER_TOPK)
        cand, cand_ids = _pair_candidates(v1, v2)
        sv, si = _topk_rows(cand, cand_ids, PEER_TOPK)
        e1 = _select_rows(i1, lax.shift_right_logical(si, 4))
        e2 = _select_rows(i2, si & (PEER_TOPK - 1))
        expert = e1 * N_KEYS + e2
        ex = jnp.exp(sv - jnp.max(sv, axis=0, keepdims=True))
        rows = slice(hd * PEER_TOPK, (hd + 1) * PEER_TOPK)
        tix_ref[rows, :] = lax.shift_right_logical(expert, 1) * SUBLANES
        par_ref[rows, :] = expert & 1
        g_ref[rows, :] = ex / jnp.sum(ex, axis=0, keepdims=True)


def _retrieve(h2, wq, k1, k2, tk):
    n = h2.shape[0]
    ospec = pl.BlockSpec((N_PICK, tk), lambda i: (0, i))
    return pl.pallas_call(
        _retrieve_kernel,
        grid=(n // tk,),
        in_specs=[pl.BlockSpec((tk, D_MODEL), lambda i: (i, 0)),
                  pl.BlockSpec((D_MODEL, PEER_HEADS * D_QUERY), lambda i: (0, 0)),
                  pl.BlockSpec((N_KEYS, D_QUERY // 2), lambda i: (0, 0)),
                  pl.BlockSpec((N_KEYS, D_QUERY // 2), lambda i: (0, 0))],
        out_specs=[ospec, ospec, ospec],
        out_shape=[jax.ShapeDtypeStruct((N_PICK, n), jnp.int32),
                   jax.ShapeDtypeStruct((N_PICK, n), jnp.int32),
                   jax.ShapeDtypeStruct((N_PICK, n), F32)],
        compiler_params=pltpu.CompilerParams(dimension_semantics=("arbitrary",)),
        name="retrieve",
    )(h2, wq, k1, k2)


TILE_ROWS = 2 * SUBLANES
IDX_PARTS = 2
KG = N_PICK * TILE_ROWS


def _tile_table(tbl):
    halves = lax.bitcast_convert_type(tbl.astype(BF16), jnp.uint16).astype(jnp.uint32)
    halves = halves.reshape(-1, SUBLANES, 2, LANES)
    words = halves[:, :, 0, :] | (halves[:, :, 1, :] << 16)
    return words.reshape(-1, LANES)


def _gathered_rows(tbl_ref, off_ref, p):
    tiles = []
    for j in range(N_PICK):
        off = pl.multiple_of(off_ref[p, j], SUBLANES)
        tiles.append(pltpu.bitcast(tbl_ref[pl.ds(off, SUBLANES), :], BF16))
    return jnp.concatenate(tiles, axis=0)


def _for_each_token(off_hbm, off_smem, sems, tb, one_token):
    step = pl.program_id(0)
    part = tb // IDX_PARTS

    def copy(block, k):
        rows = pl.ds(k * part, part)
        return pltpu.make_async_copy(off_hbm.at[pl.ds(block * tb + k * part, part)],
                                     off_smem.at[rows], sems.at[k])

    @pl.when(step == 0)
    def _():
        for k in range(IDX_PARTS):
            copy(0, k).start()

    for k in range(IDX_PARTS):
        copy(step, k).wait()
        for p in range(k * part, (k + 1) * part):
            one_token(p)

        @pl.when(step + 1 < pl.num_programs(0))
        def _():
            copy(step + 1, k).start()


def _split_bf16(x):
    hi = x.astype(BF16)
    return hi, (x - hi.astype(F32)).astype(BF16)


def _pick_patterns():
    col = jnp.arange(KG, dtype=jnp.int32)
    pick = jnp.arange(N_PICK, dtype=jnp.int32)
    expand = (col[None, :] // TILE_ROWS == pick[:, None]).astype(BF16)
    half = col[:, None] // SUBLANES
    sum_even = (half == 2 * pick[None, :]).astype(BF16)
    sum_odd = (half == 2 * pick[None, :] + 1).astype(BF16)
    return expand, sum_even, sum_odd


def _peer_act_kernel(tix_hbm, x_ref, par_ref, g_ref, sum_even_ref, sum_odd_ref, tbl_ref, w_ref,
                     xa_scr, zs_scr, tix_smem, sems):
    tb = x_ref.shape[0]
    x = x_ref[...]
    xa_scr[...] = jnp.concatenate([x, x - x.astype(BF16).astype(F32)], axis=1).astype(BF16)
    keep = ((lax.broadcasted_iota(jnp.int32, (SUBLANES, KG), 1) & (SUBLANES - 1))
            == lax.broadcasted_iota(jnp.int32, (SUBLANES, KG), 0))
    nt = (((1,), (1,)), ((), ()))

    def one_token(p):
        rows = _gathered_rows(tbl_ref, tix_smem, p)
        res = lax.dot_general(xa_scr[p], rows, nt, preferred_element_type=F32)
        z = jnp.where(keep, res[0:SUBLANES] + res[SUBLANES:], 0.0)
        zs_scr[p:p + 1, :] = jnp.sum(z, axis=0, keepdims=True)

    _for_each_token(tix_hbm, tix_smem, sems, tb, one_token)

    sum_even = sum_even_ref[...]
    sum_odd = sum_odd_ref[...]
    z_hi, z_lo = _split_bf16(zs_scr[...])
    even = (jnp.dot(z_hi, sum_even, preferred_element_type=F32)
            + jnp.dot(z_lo, sum_even, preferred_element_type=F32))
    odd = (jnp.dot(z_hi, sum_odd, preferred_element_type=F32)
           + jnp.dot(z_lo, sum_odd, preferred_element_type=F32))
    act = jnp.where(par_ref[...] == 1, odd, even)
    w_ref[...] = g_ref[...] * jax.nn.gelu(act)


def _peer_act(tix, x3, par, gates, sum_even, sum_odd, tbl, tb):
    n = tix.shape[0]
    vspec = pl.BlockSpec((tb, N_PICK), lambda i: (i, 0))
    whole = pl.BlockSpec(memory_space=pltpu.VMEM)
    return pl.pallas_call(
        _peer_act_kernel,
        grid=(n // tb,),
        in_specs=[pl.BlockSpec(memory_space=pl.ANY),
                  pl.BlockSpec((tb, SUBLANES, LANES), lambda i: (i, 0, 0)),
                  vspec, vspec, whole, whole, whole],
        out_specs=vspec,
        out_shape=jax.ShapeDtypeStruct((n, N_PICK), F32),
        scratch_shapes=[pltpu.VMEM((tb, TILE_ROWS, LANES), BF16), pltpu.VMEM((tb, KG), F32),
                        pltpu.SMEM((tb, N_PICK), jnp.int32),
                        pltpu.SemaphoreType.DMA((IDX_PARTS,))],
        compiler_params=pltpu.CompilerParams(
            dimension_semantics=("arbitrary",), vmem_limit_bytes=PEER_VMEM_LIMIT),
        name="peer_act",
    )(tix, x3, par, gates, sum_even, sum_odd, tbl)


def _peer_out_kernel(tix_hbm, w_ref, par_ref, x1_ref, g2_ref, fw_ref, expand_ref, tbl_ref, y_ref,
                     wexp_scr, tix_smem, sems):
    tb = x1_ref.shape[0]

    expand = expand_ref[...]
    w_hi, w_lo = _split_bf16(w_ref[...])
    par8 = (par_ref[...] * SUBLANES).astype(F32).astype(BF16)
    wexp_scr[0:tb, :] = jnp.dot(w_hi, expand, preferred_element_type=F32)
    wexp_scr[tb:2 * tb, :] = jnp.dot(w_lo, expand, preferred_element_type=F32)
    wexp_scr[2 * tb:3 * tb, :] = jnp.dot(par8, expand, preferred_element_type=F32)

    rel = ((lax.broadcasted_iota(jnp.int32, (SUBLANES, KG), 1) & (TILE_ROWS - 1))
           - lax.broadcasted_iota(jnp.int32, (SUBLANES, KG), 0)).astype(F32)

    def one_token(p):
        sel = rel == wexp_scr[2 * tb + p:2 * tb + p + 1, :]
        a_hi = jnp.where(sel, wexp_scr[p:p + 1, :], 0.0)
        a_lo = jnp.where(sel, wexp_scr[tb + p:tb + p + 1, :], 0.0)
        a = jnp.concatenate([a_hi, a_lo], axis=0).astype(BF16)
        rows = _gathered_rows(tbl_ref, tix_smem, p)
        res = jnp.dot(a, rows, preferred_element_type=F32)
        y_ref[p] = res[0:SUBLANES] + res[SUBLANES:]

    _for_each_token(tix_hbm, tix_smem, sems, tb, one_token)

    x2 = x1_ref[...] + g2_ref[...] * y_ref[...]
    ms = jnp.sum(jnp.sum(x2 * x2, axis=2, keepdims=True), axis=1, keepdims=True) / D_MODEL
    y_ref[...] = x2 * lax.rsqrt(ms + EPS) * fw_ref[...]


def _peer_out(tix, w, par, x1_3, g2_3, fw_3, expand, tbl, tb, tokens_per_batch):
    n = tix.shape[0]
    vspec = pl.BlockSpec((tb, N_PICK), lambda i: (i, 0))
    xspec = pl.BlockSpec((tb, SUBLANES, LANES), lambda i: (i, 0, 0))
    whole = pl.BlockSpec(memory_space=pltpu.VMEM)
    return pl.pallas_call(
        _peer_out_kernel,
        grid=(n // tb,),
        in_specs=[pl.BlockSpec(memory_space=pl.ANY),
                  vspec, vspec, xspec,
                  pl.BlockSpec((1, SUBLANES, LANES), lambda i: ((i * tb) // tokens_per_batch, 0, 0)),
                  pl.BlockSpec((SUBLANES, LANES), lambda i: (0, 0)),
                  whole, whole],
        out_specs=xspec,
        out_shape=jax.ShapeDtypeStruct((n, SUBLANES, LANES), F32),
        scratch_shapes=[pltpu.VMEM((3 * tb, KG), F32),
                        pltpu.SMEM((tb, N_PICK), jnp.int32),
                        pltpu.SemaphoreType.DMA((IDX_PARTS,))],
        compiler_params=pltpu.CompilerParams(
            dimension_semantics=("arbitrary",), vmem_limit_bytes=PEER_VMEM_LIMIT),
        name="peer_out",
    )(tix, w, par, x1_3, g2_3, fw_3, expand, tbl)


def _trunk(x, mod, h0, conv0, c0, n0, m0, chunk_len, p):
    B, T, _ = x.shape
    n = B * T
    tm = min(512, T)
    z, zg = _inproj(x, mod, p["norm1_w"], p["w_main"], p["b_main"], p["w_gate"], p["b_gate"], tm)
    ya, h_last, conv_new = _rglru(z, h0.reshape(B, 1, W_A), conv0, p["conv_w"], p["conv_b"],
                                  p["w_ri"], p["b_ri"], p["lam"], tm)
    m0b = jnp.broadcast_to(m0[:, :, None], (B, H_B, LANES))
    yb, c_new, n_new, m_new = _mlstm(z, zg, c0, n0, m0b, p["mlstm_norm_w"], chunk_len, tm, 1)
    x1, h2 = _outproj(x, ya, yb, mod, p["wo_a"], p["wo_b"], p["norm2_w"], tm)

    tk = 256
    tix_t, par_t, g_t = _retrieve(h2.reshape(n, D_MODEL), p["wq"], p["k1"], p["k2"], tk)
    tix = tix_t.T
    par = par_t.T
    tb = min(64, T)
    expand, sum_even, sum_odd = _pick_patterns()
    w = _peer_act(tix, h2.reshape(n, SUBLANES, LANES), par, g_t.T, sum_even, sum_odd,
                  p["u_tbl"], tb)
    y3 = _peer_out(tix, w, par, x1.reshape(n, SUBLANES, LANES),
                   mod[:, 5].reshape(B, SUBLANES, LANES), p["final_w"], expand, p["v_tbl"], tb, T)
    return (y3.reshape(B, T, D_MODEL), h_last.reshape(B, W_A), conv_new, c_new, n_new,
            m_new[:, :, 0])


def kernel(x_prompt, x_sample, c_prompt, c_sample, state_lru_h, state_lru_conv, state_mlstm_C,
           state_mlstm_n, state_mlstm_m, norm1_w, ada_w, ada_b, w_in, b_in, conv_w, conv_b,
           lru_wr, lru_br, lru_wi, lru_bi, lru_lambda, mlstm_norm_w, w_out, norm2_w,
           peer_wq, peer_k1, peer_k2, peer_u, peer_v, final_norm_w):
    depth = w_in.shape[0]
    assert depth == 1, "single-layer trunk only"
    B = x_prompt.shape[0]
    Bs = x_sample.shape[0]
    l = 0

    def block_diag(w):
        nb, c, d = w.shape
        eye = jnp.eye(nb, dtype=w.dtype)
        return jnp.einsum("ncd,nm->ncmd", w, eye).reshape(nb * c, nb * d)

    pad_g = LANES - N_GATE
    p = {
        "norm1_w": norm1_w[l].reshape(1, D_MODEL),
        "w_main": w_in[l][:, :N_MAIN].astype(BF16),
        "b_main": b_in[l][:N_MAIN].reshape(1, N_MAIN),
        "w_gate": jnp.pad(w_in[l][:, N_MAIN:], ((0, 0), (0, pad_g))),
        "b_gate": jnp.pad(b_in[l][N_MAIN:], (0, pad_g)).reshape(1, LANES),
        "conv_w": conv_w[l],
        "conv_b": conv_b[l].reshape(1, W_A),
        "w_ri": jnp.concatenate([block_diag(lru_wr[l]), block_diag(lru_wi[l])], axis=1).astype(BF16),
        "b_ri": jnp.concatenate([lru_br[l], lru_bi[l]]).reshape(1, 2 * W_A),
        "lam": lru_lambda[l].reshape(1, W_A),
        "mlstm_norm_w": mlstm_norm_w[l].reshape(1, W_B),
        "wo_a": w_out[l][:W_A].astype(BF16),
        "wo_b": w_out[l][W_A:].astype(BF16),
        "norm2_w": norm2_w[l].reshape(1, D_MODEL),
        "wq": peer_wq[l].astype(BF16),
        "k1": peer_k1[l],
        "k2": peer_k2[l],
        "u_tbl": _tile_table(peer_u[l]),
        "v_tbl": _tile_table(peer_v[l]),
        "final_w": final_norm_w.reshape(SUBLANES, LANES),
    }

    mod = _adaln(jnp.concatenate([c_prompt, c_sample], axis=0), ada_w[l], ada_b[l])
    mod = mod.reshape(B + Bs, N_ADA, D_MODEL)

    dt = x_prompt.dtype
    outs_p = _trunk(x_prompt, mod[:B], jnp.zeros((B, W_A), dt), jnp.zeros((B, CONV_W - 1, W_A), dt),
                    jnp.zeros((B, H_B, DK, DK), dt), jnp.zeros((B, H_B, DK), dt),
                    jnp.zeros((B, H_B), dt), CHUNK, p)
    outs_s = _trunk(x_sample, mod[B:], state_lru_h[l], state_lru_conv[l], state_mlstm_C[l],
                    state_mlstm_n[l], state_mlstm_m[l], x_sample.shape[1], p)
    y_p, *st_p = outs_p
    y_s, *st_s = outs_s
    st_p = [s[None].astype(r.dtype) for s, r in zip(
        st_p, (state_lru_h, state_lru_conv, state_mlstm_C, state_mlstm_n, state_mlstm_m))]
    st_s = [s[None].astype(r.dtype) for s, r in zip(
        st_s, (state_lru_h, state_lru_conv, state_mlstm_C, state_mlstm_n, state_mlstm_m))]
    return (y_p, y_s, *st_p, *st_s)
```

```python
import functools

import jax
import jax.numpy as jnp
from jax import lax
from jax.experimental import pallas as pl
from jax.experimental.pallas import tpu as pltpu

D_MODEL = 1024
W_A = 512
W_B = 512
LRU_BLOCKS = 8
CONV_W = 4
LRU_C = 8.0
H_B = 4
DK = W_B // H_B
N_ADA = 6
PEER_HEADS = 8
N_KEYS = 128
D_QUERY = 256
PEER_TOPK = 16
N_PICK = PEER_HEADS * PEER_TOPK
EPS = 1e-6
CHUNK = 64
N_MAIN = 2 * W_A + 4 * W_B
N_GATE = 2 * H_B

LANES = 128
SUBLANES = 8
PEER_VMEM_LIMIT = 48 * 1024 * 1024

F32 = jnp.float32
BF16 = jnp.bfloat16
HIGHEST = lax.Precision.HIGHEST


def _rms(x, w):
    return x * lax.rsqrt(jnp.mean(x * x, axis=-1, keepdims=True) + EPS) * w


def _adaln_kernel(c_ref, w_ref, b_ref, o_ref):
    s = jax.nn.silu(c_ref[...])
    o_ref[...] = jnp.dot(s, w_ref[...], preferred_element_type=F32, precision=HIGHEST) + b_ref[...]


def _adaln(c, w, b):
    nb = c.shape[0]
    n_out = w.shape[1]
    tn = 1024
    return pl.pallas_call(
        _adaln_kernel,
        grid=(n_out // tn,),
        in_specs=[pl.BlockSpec((nb, D_MODEL), lambda j: (0, 0)),
                  pl.BlockSpec((D_MODEL, tn), lambda j: (0, j)),
                  pl.BlockSpec((1, tn), lambda j: (0, j))],
        out_specs=pl.BlockSpec((nb, tn), lambda j: (0, j)),
        out_shape=jax.ShapeDtypeStruct((nb, n_out), F32),
        name="adaln",
    )(c, w, b.reshape(1, n_out))


def _inproj_kernel(x_ref, mod_ref, nw_ref, w_ref, b_ref, wg_ref, bg_ref, z_ref, zg_ref):
    h = _rms(x_ref[0], nw_ref[...]) * (1.0 + mod_ref[0, 1:2, :]) + mod_ref[0, 0:1, :]
    z_ref[0] = jnp.dot(h.astype(BF16), w_ref[...], preferred_element_type=F32) + b_ref[...]
    zg_ref[0] = jnp.dot(h, wg_ref[...], preferred_element_type=F32, precision=HIGHEST) + bg_ref[...]


def _inproj(x, mod, nw, w_main, b_main, w_gate, b_gate, tm):
    B, T, _ = x.shape
    return pl.pallas_call(
        _inproj_kernel,
        grid=(B, T // tm),
        in_specs=[pl.BlockSpec((1, tm, D_MODEL), lambda b, t: (b, t, 0)),
                  pl.BlockSpec((1, N_ADA, D_MODEL), lambda b, t: (b, 0, 0)),
                  pl.BlockSpec((1, D_MODEL), lambda b, t: (0, 0)),
                  pl.BlockSpec((D_MODEL, N_MAIN), lambda b, t: (0, 0)),
                  pl.BlockSpec((1, N_MAIN), lambda b, t: (0, 0)),
                  pl.BlockSpec((D_MODEL, LANES), lambda b, t: (0, 0)),
                  pl.BlockSpec((1, LANES), lambda b, t: (0, 0))],
        out_specs=[pl.BlockSpec((1, tm, N_MAIN), lambda b, t: (b, t, 0)),
                   pl.BlockSpec((1, tm, LANES), lambda b, t: (b, t, 0))],
        out_shape=[jax.ShapeDtypeStruct((B, T, N_MAIN), F32),
                   jax.ShapeDtypeStruct((B, T, LANES), F32)],
        compiler_params=pltpu.CompilerParams(
            dimension_semantics=("arbitrary", "arbitrary"), vmem_limit_bytes=48 * 1024 * 1024),
        name="inproj",
    )(x, mod, nw, w_main, b_main, w_gate, b_gate)


def _rglru_kernel(z_ref, h0_ref, conv0_ref, cw_ref, cb_ref, wri_ref, bri_ref, lam_ref,
                  ya_ref, hlast_ref, convnew_ref, xbuf, hcar):
    t = pl.program_id(1)
    tt = z_ref.shape[1]

    @pl.when(t == 0)
    def _():
        xbuf[0:SUBLANES, :] = jnp.zeros((SUBLANES, W_A), F32)
        xbuf[SUBLANES - (CONV_W - 1):SUBLANES, :] = conv0_ref[0]
        hcar[...] = h0_ref[0]

    xbuf[SUBLANES:SUBLANES + tt, :] = z_ref[0, :, 0:W_A]
    u = cb_ref[...]
    for k in range(CONV_W):
        off = SUBLANES - (CONV_W - 1) + k
        u = u + xbuf[off:off + tt, :] * cw_ref[k:k + 1, :]
    convnew_ref[0] = xbuf[tt + SUBLANES - (CONV_W - 1):tt + SUBLANES, :]
    xbuf[0:SUBLANES, :] = xbuf[tt:tt + SUBLANES, :]

    ri = jnp.dot(u.astype(BF16), wri_ref[...], preferred_element_type=F32) + bri_ref[...]
    r = jax.nn.sigmoid(ri[:, 0:W_A])
    i = jax.nn.sigmoid(ri[:, W_A:2 * W_A])
    log_a = (-LRU_C) * r * jax.nn.softplus(-lam_ref[...])
    a = jnp.exp(log_a)
    bx = jnp.sqrt(jnp.tanh(-log_a) * (a * a + 1.0)) * (i * u)

    row = lax.broadcasted_iota(jnp.int32, (tt, W_A), 0)
    d = 1
    while d < tt:
        keep = row >= d
        a_sh = jnp.where(keep, pltpu.roll(a, d, 0), 1.0)
        b_sh = jnp.where(keep, pltpu.roll(bx, d, 0), 0.0)
        bx = bx + a * b_sh
        a = a * a_sh
        d *= 2
    h = bx + a * hcar[...]
    hcar[...] = h[tt - 1:tt, :]
    hlast_ref[0] = h[tt - 1:tt, :]
    ya_ref[0] = (h * jax.nn.gelu(z_ref[0, :, W_A:2 * W_A])).astype(ya_ref.dtype)


def _rglru(z, h0, conv0, conv_w, conv_b, w_ri, b_ri, lam, tt):
    B, T, _ = z.shape
    return pl.pallas_call(
        _rglru_kernel,
        grid=(B, T // tt),
        in_specs=[pl.BlockSpec((1, tt, 2 * W_A), lambda b, t: (b, t, 0)),
                  pl.BlockSpec((1, 1, W_A), lambda b, t: (b, 0, 0)),
                  pl.BlockSpec((1, CONV_W - 1, W_A), lambda b, t: (b, 0, 0)),
                  pl.BlockSpec((CONV_W, W_A), lambda b, t: (0, 0)),
                  pl.BlockSpec((1, W_A), lambda b, t: (0, 0)),
                  pl.BlockSpec((W_A, 2 * W_A), lambda b, t: (0, 0)),
                  pl.BlockSpec((1, 2 * W_A), lambda b, t: (0, 0)),
                  pl.BlockSpec((1, W_A), lambda b, t: (0, 0))],
        out_specs=[pl.BlockSpec((1, tt, W_A), lambda b, t: (b, t, 0)),
                   pl.BlockSpec((1, 1, W_A), lambda b, t: (b, 0, 0)),
                   pl.BlockSpec((1, CONV_W - 1, W_A), lambda b, t: (b, 0, 0))],
        out_shape=[jax.ShapeDtypeStruct((B, T, W_A), BF16),
                   jax.ShapeDtypeStruct((B, 1, W_A), F32),
                   jax.ShapeDtypeStruct((B, CONV_W - 1, W_A), F32)],
        scratch_shapes=[pltpu.VMEM((tt + SUBLANES, W_A), F32), pltpu.VMEM((1, W_A), F32)],
        compiler_params=pltpu.CompilerParams(dimension_semantics=("arbitrary", "arbitrary")),
        name="rglru",
    )(z, h0, conv0, conv_w, conv_b, w_ri, b_ri, lam)


def _mlstm_kernel(L, zqk_ref, zvo_ref, zg_ref, c0_ref, n0_ref, m0_ref, nw_ref,
                  yb_ref, cout_ref, nout_ref, mout_ref, c_s, n_s, m_s):
    t = pl.program_id(1)
    nb, tt, _ = zqk_ref.shape

    @pl.when(t == 0)
    def _():
        c_s[...] = c0_ref[...]
        n_s[...] = n0_ref[...]
        m_s[...] = m0_ref[...]

    ri = lax.broadcasted_iota(jnp.int32, (L, L), 0)
    ci = lax.broadcasted_iota(jnp.int32, (L, L), 1)
    causal = ci <= ri
    tri = causal.astype(F32)
    eye = (ci == ri).astype(F32)

    def to_row(col):
        return jnp.sum(col * eye, axis=0, keepdims=True)

    def chunk(c, carry):
        r0 = pl.multiple_of(c * L, L)
        rows = pl.ds(r0, L)
        for bi, hd in [(bi, hd) for bi in range(nb) for hd in range(H_B)]:
            g = zg_ref[bi, rows, :]
            cols = slice(hd * DK, (hd + 1) * DK)
            q = zqk_ref[bi, rows, hd * DK:(hd + 1) * DK] * (DK ** -0.5)
            k = zqk_ref[bi, rows, W_B + hd * DK:W_B + (hd + 1) * DK]
            v = zvo_ref[bi, rows, hd * DK:(hd + 1) * DK]
            o = zvo_ref[bi, rows, W_B + hd * DK:W_B + (hd + 1) * DK]
            ig_c = g[:, hd:hd + 1]
            lf_c = jax.nn.log_sigmoid(g[:, H_B + hd:H_B + hd + 1])
            ig_r = to_row(ig_c)
            b_c = jnp.sum(tri * to_row(lf_c), axis=1, keepdims=True)
            b_r = to_row(b_c)
            m_prev = m_s[bi, hd:hd + 1, 0:1]
            n_prev = n_s[bi, hd:hd + 1, :]
            c_prev = c_s[bi, hd]

            log_d = jnp.where(causal, b_c - b_r + ig_r, -jnp.inf)
            inter = b_c + m_prev
            m_t = jnp.maximum(inter, jnp.max(log_d, axis=1, keepdims=True))
            dmat = jnp.exp(log_d - m_t)
            w_state = jnp.exp(inter - m_t)
            qb = q.astype(BF16)
            s = lax.dot_general(qb, k.astype(BF16), (((1,), (1,)), ((), ())),
                                preferred_element_type=F32) * dmat
            num = (jnp.dot(s.astype(BF16), v.astype(BF16), preferred_element_type=F32)
                   + w_state * jnp.dot(qb, c_prev.astype(BF16), preferred_element_type=F32))
            den = (jnp.sum(s, axis=1, keepdims=True)
                   + w_state * jnp.sum(q * n_prev, axis=1, keepdims=True))
            h = num / jnp.maximum(jnp.abs(den), jnp.exp(-m_t))

            m_new = m_t[L - 1:L, :]
            w_k = jnp.exp(b_c[L - 1:L, :] - b_c + ig_c - m_new)
            w_c = jnp.exp(inter[L - 1:L, :] - m_new)
            kw = k * w_k
            c_s[bi, hd] = w_c * c_prev + jnp.dot(kw.T.astype(BF16), v.astype(BF16),
                                                 preferred_element_type=F32)
            n_s[bi, hd:hd + 1, :] = w_c * n_prev + jnp.sum(kw, axis=0, keepdims=True)
            m_s[bi, hd:hd + 1, :] = jnp.broadcast_to(m_new, (1, LANES))

            hn = h * lax.rsqrt(jnp.mean(h * h, axis=1, keepdims=True) + EPS)
            y = jax.nn.sigmoid(o) * (hn * nw_ref[:, cols])
            yb_ref[bi, rows, hd * DK:(hd + 1) * DK] = y.astype(yb_ref.dtype)
        return carry

    lax.fori_loop(0, tt // L, chunk, 0)
    cout_ref[...] = c_s[...]
    nout_ref[...] = n_s[...]
    mout_ref[...] = m_s[...]


def _mlstm(z, zg, c0, n0, m0, norm_w, L, tt, nb):
    B, T, _ = z.shape
    state_specs = [pl.BlockSpec((nb, H_B, DK, DK), lambda b, t: (b, 0, 0, 0)),
                   pl.BlockSpec((nb, H_B, DK), lambda b, t: (b, 0, 0)),
                   pl.BlockSpec((nb, H_B, LANES), lambda b, t: (b, 0, 0))]
    return pl.pallas_call(
        functools.partial(_mlstm_kernel, L),
        grid=(B // nb, T // tt),
        in_specs=[pl.BlockSpec((nb, tt, 2 * W_B), lambda b, t: (b, t, 1)),
                  pl.BlockSpec((nb, tt, 2 * W_B), lambda b, t: (b, t, 2)),
                  pl.BlockSpec((nb, tt, LANES), lambda b, t: (b, t, 0))]
        + state_specs + [pl.BlockSpec((1, W_B), lambda b, t: (0, 0))],
        out_specs=[pl.BlockSpec((nb, tt, W_B), lambda b, t: (b, t, 0))] + state_specs,
        out_shape=[jax.ShapeDtypeStruct((B, T, W_B), BF16),
                   jax.ShapeDtypeStruct((B, H_B, DK, DK), F32),
                   jax.ShapeDtypeStruct((B, H_B, DK), F32),
                   jax.ShapeDtypeStruct((B, H_B, LANES), F32)],
        scratch_shapes=[pltpu.VMEM((nb, H_B, DK, DK), F32), pltpu.VMEM((nb, H_B, DK), F32),
                        pltpu.VMEM((nb, H_B, LANES), F32)],
        compiler_params=pltpu.CompilerParams(dimension_semantics=("arbitrary", "arbitrary")),
        name="mlstm",
    )(z, z, zg, c0, n0, m0, norm_w)


def _outproj_kernel(x_ref, ya_ref, yb_ref, mod_ref, woa_ref, wob_ref, nw_ref, x1_ref, h2_ref):
    mix = (jnp.dot(ya_ref[0], woa_ref[...], preferred_element_type=F32)
           + jnp.dot(yb_ref[0], wob_ref[...], preferred_element_type=F32))
    x1 = x_ref[0] + mod_ref[0, 2:3, :] * mix
    x1_ref[0] = x1
    h2_ref[0] = _rms(x1, nw_ref[...]) * (1.0 + mod_ref[0, 4:5, :]) + mod_ref[0, 3:4, :]


def _outproj(x, ya, yb, mod, wo_a, wo_b, nw, tm):
    B, T, _ = x.shape
    xspec = pl.BlockSpec((1, tm, D_MODEL), lambda b, t: (b, t, 0))
    yspec = pl.BlockSpec((1, tm, W_A), lambda b, t: (b, t, 0))
    return pl.pallas_call(
        _outproj_kernel,
        grid=(B, T // tm),
        in_specs=[xspec, yspec, yspec,
                  pl.BlockSpec((1, N_ADA, D_MODEL), lambda b, t: (b, 0, 0)),
                  pl.BlockSpec((W_A, D_MODEL), lambda b, t: (0, 0)),
                  pl.BlockSpec((W_B, D_MODEL), lambda b, t: (0, 0)),
                  pl.BlockSpec((1, D_MODEL), lambda b, t: (0, 0))],
        out_specs=[xspec, xspec],
        out_shape=[jax.ShapeDtypeStruct((B, T, D_MODEL), F32)] * 2,
        compiler_params=pltpu.CompilerParams(dimension_semantics=("arbitrary", "arbitrary")),
        name="outproj",
    )(x, ya, yb, mod, wo_a, wo_b, nw)


def _topk_rows(s, ids, k):
    ids = ids.astype(F32)
    big = jnp.float32(2 ** 24)
    slot = lax.broadcasted_iota(jnp.int32, (k, s.shape[1]), 0)
    vals = jnp.zeros((k, s.shape[1]), F32)
    idxs = jnp.zeros((k, s.shape[1]), F32)
    for j in range(k):
        m = jnp.max(s, axis=0, keepdims=True)
        i = jnp.min(jnp.where(s == m, ids, big), axis=0, keepdims=True)
        s = jnp.where(ids == i, -jnp.inf, s)
        vals = jnp.where(slot == j, m, vals)
        idxs = jnp.where(slot == j, i, idxs)
    return vals, idxs.astype(jnp.int32)


def _pair_candidates(v1, v2):
    k, tk = v1.shape
    row = lax.broadcasted_iota(jnp.int32, (SUBLANES, tk), 0)
    sums = [v1[0:1, :] + v2]
    ids = [lax.broadcasted_iota(jnp.int32, (k, tk), 0)]
    for i in range(1, SUBLANES):
        blk = v1[i:i + 1, :] + v2[0:SUBLANES, :]
        sums.append(jnp.where(row < k // (i + 1), blk, -jnp.inf))
        ids.append(row + i * k)
    for i0 in range(SUBLANES, k, SUBLANES):
        sums.append(v1[i0:i0 + SUBLANES, :] + v2[0:1, :])
        ids.append((row + i0) * k)
    return jnp.concatenate(sums, axis=0), jnp.concatenate(ids, axis=0)


def _select_rows(tbl, sel):
    out = jnp.zeros_like(tbl)
    for i in range(tbl.shape[0]):
        out = jnp.where(sel == i, tbl[i:i + 1, :], out)
    return out


def _retrieve_kernel(h2_ref, wq_ref, k1_ref, k2_ref, tix_ref, par_ref, g_ref):
    q = jnp.dot(h2_ref[...].astype(BF16), wq_ref[...], preferred_element_type=F32)
    half = D_QUERY // 2
    nt = (((1,), (1,)), ((), ()))
    key_ids = lax.broadcasted_iota(jnp.int32, (N_KEYS, q.shape[0]), 0).astype(F32)
    for hd in range(PEER_HEADS):
        qh = q[:, hd * D_QUERY:(hd + 1) * D_QUERY]
        qh = qh * lax.rsqrt(jnp.mean(qh * qh, axis=1, keepdims=True) + EPS)
        s1 = lax.dot_general(k1_ref[...], qh[:, :half], nt, preferred_element_type=F32,
                             precision=HIGHEST)
        s2 = lax.dot_general(k2_ref[...], qh[:, half:], nt, preferred_element_type=F32,
                             precision=HIGHEST)
        v1, i1 = _topk_rows(s1, key_ids, PEER_TOPK)
        v2, i2 = _topk_rows(s2, key_ids, PEER_TOPK)
        cand, cand_ids = _pair_candidates(v1, v2)
        sv, si = _topk_rows(cand, cand_ids, PEER_TOPK)
        e1 = _select_rows(i1, lax.shift_right_logical(si, 4))
        e2 = _select_rows(i2, si & (PEER_TOPK - 1))
        expert = e1 * N_KEYS + e2
        ex = jnp.exp(sv - jnp.max(sv, axis=0, keepdims=True))
        rows = slice(hd * PEER_TOPK, (hd + 1) * PEER_TOPK)
        tix_ref[rows, :] = lax.shift_right_logical(expert, 1) * SUBLANES
        par_ref[rows, :] = expert & 1
        g_ref[rows, :] = ex / jnp.sum(ex, axis=0, keepdims=True)


def _retrieve(h2, wq, k1, k2, tk):
    n = h2.shape[0]
    ospec = pl.BlockSpec((N_PICK, tk), lambda i: (0, i))
    return pl.pallas_call(
        _retrieve_kernel,
        grid=(n // tk,),
        in_specs=[pl.BlockSpec((tk, D_MODEL), lambda i: (i, 0)),
                  pl.BlockSpec((D_MODEL, PEER_HEADS * D_QUERY), lambda i: (0, 0)),
                  pl.BlockSpec((N_KEYS, D_QUERY // 2), lambda i: (0, 0)),
                  pl.BlockSpec((N_KEYS, D_QUERY // 2), lambda i: (0, 0))],
        out_specs=[ospec, ospec, ospec],
        out_shape=[jax.ShapeDtypeStruct((N_PICK, n), jnp.int32),
                   jax.ShapeDtypeStruct((N_PICK, n), jnp.int32),
                   jax.ShapeDtypeStruct((N_PICK, n), F32)],
        compiler_params=pltpu.CompilerParams(dimension_semantics=("arbitrary",)),
        name="retrieve",
    )(h2, wq, k1, k2)


TILE_ROWS = 2 * SUBLANES
IDX_PARTS = 2
KG = N_PICK * TILE_ROWS


def _tile_table(tbl):
    halves = lax.bitcast_convert_type(tbl.astype(BF16), jnp.uint16).astype(jnp.uint32)
    halves = halves.reshape(-1, SUBLANES, 2, LANES)
    words = halves[:, :, 0, :] | (halves[:, :, 1, :] << 16)
    return words.reshape(-1, LANES)


def _gathered_rows(tbl_ref, off_ref, p):
    tiles = []
    for j in range(N_PICK):
        off = pl.multiple_of(off_ref[p, j], SUBLANES)
        tiles.append(pltpu.bitcast(tbl_ref[pl.ds(off, SUBLANES), :], BF16))
    return jnp.concatenate(tiles, axis=0)


def _for_each_token(off_hbm, off_smem, sems, tb, one_token):
    step = pl.program_id(0)
    part = tb // IDX_PARTS

    def copy(block, k):
        rows = pl.ds(k * part, part)
        return pltpu.make_async_copy(off_hbm.at[pl.ds(block * tb + k * part, part)],
                                     off_smem.at[rows], sems.at[k])

    @pl.when(step == 0)
    def _():
        for k in range(IDX_PARTS):
            copy(0, k).start()

    for k in range(IDX_PARTS):
        copy(step, k).wait()
        for p in range(k * part, (k + 1) * part):
            one_token(p)

        @pl.when(step + 1 < pl.num_programs(0))
        def _():
            copy(step + 1, k).start()


def _split_bf16(x):
    hi = x.astype(BF16)
    return hi, (x - hi.astype(F32)).astype(BF16)


def _pick_patterns():
    col = jnp.arange(KG, dtype=jnp.int32)
    pick = jnp.arange(N_PICK, dtype=jnp.int32)
    expand = (col[None, :] // TILE_ROWS == pick[:, None]).astype(BF16)
    half = col[:, None] // SUBLANES
    sum_even = (half == 2 * pick[None, :]).astype(BF16)
    sum_odd = (half == 2 * pick[None, :] + 1).astype(BF16)
    return expand, sum_even, sum_odd


N_MXU = N_PICK // 2
KM = N_MXU * TILE_ROWS


def _peer_act_kernel(tix_hbm, x_ref, par_ref, g_ref, sum_even_ref, sum_odd_ref, tbl_ref, w_ref,
                     xa_scr, zs_scr, r_scr, actv_scr, tix_smem, sems):
    tb = x_ref.shape[0]
    x = x_ref[...]
    xa_scr[...] = jnp.concatenate([x, x - x.astype(BF16).astype(F32)], axis=1).astype(BF16)
    r_scr[...] = jnp.zeros_like(r_scr)
    keep = ((lax.broadcasted_iota(jnp.int32, (SUBLANES, KM), 1) & (SUBLANES - 1))
            == lax.broadcasted_iota(jnp.int32, (SUBLANES, KM), 0))
    nt = (((1,), (1,)), ((), ()))
    half = SUBLANES // 2

    def one_token(p):
        tiles = []
        for j in range(N_MXU):
            off = pl.multiple_of(tix_smem[p, j], SUBLANES)
            tiles.append(pltpu.bitcast(tbl_ref[pl.ds(off, SUBLANES), :], BF16))
        rows = jnp.concatenate(tiles, axis=0)
        res = lax.dot_general(xa_scr[p], rows, nt, preferred_element_type=F32)
        z = jnp.where(keep, res[0:SUBLANES] + res[SUBLANES:], 0.0)
        zs_scr[p:p + 1, :] = jnp.sum(z, axis=0, keepdims=True)

        xp = x_ref[p]
        x_even = jnp.concatenate([xp[c:c + 1] for c in (0, 2, 4, 6)] * 2, axis=0)
        x_odd = jnp.concatenate([xp[c:c + 1] for c in (1, 3, 5, 7)] * 2, axis=0)
        for j in range(N_MXU, N_PICK):
            off = pl.multiple_of(tix_smem[p, j], SUBLANES)
            word = tbl_ref[pl.ds(off, SUBLANES), :]
            rows_even = lax.bitcast_convert_type(word << 16, F32)
            rows_odd = lax.bitcast_convert_type(word & jnp.uint32(0xFFFF0000), F32)
            r_scr[:, j:j + 1] = jnp.sum(rows_even * x_even + rows_odd * x_odd, axis=1, keepdims=True)
        r = r_scr[...]
        first = jnp.sum(r[0:half], axis=0, keepdims=True)
        second = jnp.sum(r[half:], axis=0, keepdims=True)
        actv_scr[p:p + 1, :] = jnp.where(par_ref[p:p + 1, :] == 1, second, first)

    _for_each_token(tix_hbm, tix_smem, sems, tb, one_token)

    sum_even = sum_even_ref[0:KM, :]
    sum_odd = sum_odd_ref[0:KM, :]
    z_hi, z_lo = _split_bf16(zs_scr[...])
    even = (jnp.dot(z_hi, sum_even, preferred_element_type=F32)
            + jnp.dot(z_lo, sum_even, preferred_element_type=F32))
    odd = (jnp.dot(z_hi, sum_odd, preferred_element_type=F32)
           + jnp.dot(z_lo, sum_odd, preferred_element_type=F32))
    act = jnp.where(par_ref[...] == 1, odd, even) + actv_scr[...]
    w_ref[...] = g_ref[...] * jax.nn.gelu(act)


def _peer_act(tix, x3, par, gates, sum_even, sum_odd, tbl, tb):
    n = tix.shape[0]
    vspec = pl.BlockSpec((tb, N_PICK), lambda i: (i, 0))
    whole = pl.BlockSpec(memory_space=pltpu.VMEM)
    return pl.pallas_call(
        _peer_act_kernel,
        grid=(n // tb,),
        in_specs=[pl.BlockSpec(memory_space=pl.ANY),
                  pl.BlockSpec((tb, SUBLANES, LANES), lambda i: (i, 0, 0)),
                  vspec, vspec, whole, whole, whole],
        out_specs=vspec,
        out_shape=jax.ShapeDtypeStruct((n, N_PICK), F32),
        scratch_shapes=[pltpu.VMEM((tb, TILE_ROWS, LANES), BF16), pltpu.VMEM((tb, KM), F32),
                        pltpu.VMEM((SUBLANES, N_PICK), F32), pltpu.VMEM((tb, N_PICK), F32),
                        pltpu.SMEM((tb, N_PICK), jnp.int32),
                        pltpu.SemaphoreType.DMA((IDX_PARTS,))],
        compiler_params=pltpu.CompilerParams(
            dimension_semantics=("arbitrary",), vmem_limit_bytes=PEER_VMEM_LIMIT),
        name="peer_act",
    )(tix, x3, par, gates, sum_even, sum_odd, tbl)


def _peer_out_kernel(tix_hbm, w_ref, par_ref, x1_ref, g2_ref, fw_ref, expand_ref, tbl_ref, y_ref,
                     wexp_scr, tix_smem, sems):
    tb = x1_ref.shape[0]

    expand = expand_ref[...]
    w_hi, w_lo = _split_bf16(w_ref[...])
    par8 = (par_ref[...] * SUBLANES).astype(F32).astype(BF16)
    wexp_scr[0:tb, :] = jnp.dot(w_hi, expand, preferred_element_type=F32)
    wexp_scr[tb:2 * tb, :] = jnp.dot(w_lo, expand, preferred_element_type=F32)
    wexp_scr[2 * tb:3 * tb, :] = jnp.dot(par8, expand, preferred_element_type=F32)

    rel = ((lax.broadcasted_iota(jnp.int32, (SUBLANES, KG), 1) & (TILE_ROWS - 1))
           - lax.broadcasted_iota(jnp.int32, (SUBLANES, KG), 0)).astype(F32)

    def one_token(p):
        sel = rel == wexp_scr[2 * tb + p:2 * tb + p + 1, :]
        a_hi = jnp.where(sel, wexp_scr[p:p + 1, :], 0.0)
        a_lo = jnp.where(sel, wexp_scr[tb + p:tb + p + 1, :], 0.0)
        a = jnp.concatenate([a_hi, a_lo], axis=0).astype(BF16)
        rows = _gathered_rows(tbl_ref, tix_smem, p)
        res = jnp.dot(a, rows, preferred_element_type=F32)
        y_ref[p] = res[0:SUBLANES] + res[SUBLANES:]

    _for_each_token(tix_hbm, tix_smem, sems, tb, one_token)

    x2 = x1_ref[...] + g2_ref[...] * y_ref[...]
    ms = jnp.sum(jnp.sum(x2 * x2, axis=2, keepdims=True), axis=1, keepdims=True) / D_MODEL
    y_ref[...] = x2 * lax.rsqrt(ms + EPS) * fw_ref[...]


def _peer_out(tix, w, par, x1_3, g2_3, fw_3, expand, tbl, tb, tokens_per_batch):
    n = tix.shape[0]
    vspec = pl.BlockSpec((tb, N_PICK), lambda i: (i, 0))
    xspec = pl.BlockSpec((tb, SUBLANES, LANES), lambda i: (i, 0, 0))
    whole = pl.BlockSpec(memory_space=pltpu.VMEM)
    return pl.pallas_call(
        _peer_out_kernel,
        grid=(n // tb,),
        in_specs=[pl.BlockSpec(memory_space=pl.ANY),
                  vspec, vspec, xspec,
                  pl.BlockSpec((1, SUBLANES, LANES), lambda i: ((i * tb) // tokens_per_batch, 0, 0)),
                  pl.BlockSpec((SUBLANES, LANES), lambda i: (0, 0)),
                  whole, whole],
        out_specs=xspec,
        out_shape=jax.ShapeDtypeStruct((n, SUBLANES, LANES), F32),
        scratch_shapes=[pltpu.VMEM((3 * tb, KG), F32),
                        pltpu.SMEM((tb, N_PICK), jnp.int32),
                        pltpu.SemaphoreType.DMA((IDX_PARTS,))],
        compiler_params=pltpu.CompilerParams(
            dimension_semantics=("arbitrary",), vmem_limit_bytes=PEER_VMEM_LIMIT),
        name="peer_out",
    )(tix, w, par, x1_3, g2_3, fw_3, expand, tbl)


def _trunk(x, mod, h0, conv0, c0, n0, m0, chunk_len, p):
    B, T, _ = x.shape
    n = B * T
    tm = min(512, T)
    z, zg = _inproj(x, mod, p["norm1_w"], p["w_main"], p["b_main"], p["w_gate"], p["b_gate"], tm)
    ya, h_last, conv_new = _rglru(z, h0.reshape(B, 1, W_A), conv0, p["conv_w"], p["conv_b"],
                                  p["w_ri"], p["b_ri"], p["lam"], tm)
    m0b = jnp.broadcast_to(m0[:, :, None], (B, H_B, LANES))
    yb, c_new, n_new, m_new = _mlstm(z, zg, c0, n0, m0b, p["mlstm_norm_w"], chunk_len, tm, 1)
    x1, h2 = _outproj(x, ya, yb, mod, p["wo_a"], p["wo_b"], p["norm2_w"], tm)

    tk = 256
    tix_t, par_t, g_t = _retrieve(h2.reshape(n, D_MODEL), p["wq"], p["k1"], p["k2"], tk)
    tix = tix_t.T
    par = par_t.T
    tb = min(64, T)
    expand, sum_even, sum_odd = _pick_patterns()
    w = _peer_act(tix, h2.reshape(n, SUBLANES, LANES), par, g_t.T, sum_even, sum_odd,
                  p["u_tbl"], tb)
    y3 = _peer_out(tix, w, par, x1.reshape(n, SUBLANES, LANES),
                   mod[:, 5].reshape(B, SUBLANES, LANES), p["final_w"], expand, p["v_tbl"], tb, T)
    return (y3.reshape(B, T, D_MODEL), h_last.reshape(B, W_A), conv_new, c_new, n_new,
            m_new[:, :, 0])


def kernel(x_prompt, x_sample, c_prompt, c_sample, state_lru_h, state_lru_conv, state_mlstm_C,
           state_mlstm_n, state_mlstm_m, norm1_w, ada_w, ada_b, w_in, b_in, conv_w, conv_b,
           lru_wr, lru_br, lru_wi, lru_bi, lru_lambda, mlstm_norm_w, w_out, norm2_w,
           peer_wq, peer_k1, peer_k2, peer_u, peer_v, final_norm_w):
    depth = w_in.shape[0]
    assert depth == 1, "single-layer trunk only"
    B = x_prompt.shape[0]
    Bs = x_sample.shape[0]
    l = 0

    def block_diag(w):
        nb, c, d = w.shape
        eye = jnp.eye(nb, dtype=w.dtype)
        return jnp.einsum("ncd,nm->ncmd", w, eye).reshape(nb * c, nb * d)

    pad_g = LANES - N_GATE
    p = {
        "norm1_w": norm1_w[l].reshape(1, D_MODEL),
        "w_main": w_in[l][:, :N_MAIN].astype(BF16),
        "b_main": b_in[l][:N_MAIN].reshape(1, N_MAIN),
        "w_gate": jnp.pad(w_in[l][:, N_MAIN:], ((0, 0), (0, pad_g))),
        "b_gate": jnp.pad(b_in[l][N_MAIN:], (0, pad_g)).reshape(1, LANES),
        "conv_w": conv_w[l],
        "conv_b": conv_b[l].reshape(1, W_A),
        "w_ri": jnp.concatenate([block_diag(lru_wr[l]), block_diag(lru_wi[l])], axis=1).astype(BF16),
        "b_ri": jnp.concatenate([lru_br[l], lru_bi[l]]).reshape(1, 2 * W_A),
        "lam": lru_lambda[l].reshape(1, W_A),
        "mlstm_norm_w": mlstm_norm_w[l].reshape(1, W_B),
        "wo_a": w_out[l][:W_A].astype(BF16),
        "wo_b": w_out[l][W_A:].astype(BF16),
        "norm2_w": norm2_w[l].reshape(1, D_MODEL),
        "wq": peer_wq[l].astype(BF16),
        "k1": peer_k1[l],
        "k2": peer_k2[l],
        "u_tbl": _tile_table(peer_u[l]),
        "v_tbl": _tile_table(peer_v[l]),
        "final_w": final_norm_w.reshape(SUBLANES, LANES),
    }

    mod = _adaln(jnp.concatenate([c_prompt, c_sample], axis=0), ada_w[l], ada_b[l])
    mod = mod.reshape(B + Bs, N_ADA, D_MODEL)

    dt = x_prompt.dtype
    outs_p = _trunk(x_prompt, mod[:B], jnp.zeros((B, W_A), dt), jnp.zeros((B, CONV_W - 1, W_A), dt),
                    jnp.zeros((B, H_B, DK, DK), dt), jnp.zeros((B, H_B, DK), dt),
                    jnp.zeros((B, H_B), dt), CHUNK, p)
    outs_s = _trunk(x_sample, mod[B:], state_lru_h[l], state_lru_conv[l], state_mlstm_C[l],
                    state_mlstm_n[l], state_mlstm_m[l], x_sample.shape[1], p)
    y_p, *st_p = outs_p
    y_s, *st_s = outs_s
    st_p = [s[None].astype(r.dtype) for s, r in zip(
        st_p, (state_lru_h, state_lru_conv, state_mlstm_C, state_mlstm_n, state_mlstm_m))]
    st_s = [s[None].astype(r.dtype) for s, r in zip(
        st_s, (state_lru_h, state_lru_conv, state_mlstm_C, state_mlstm_n, state_mlstm_m))]
    return (y_p, y_s, *st_p, *st_s)
```

```python
import functools

import jax
import jax.numpy as jnp
from jax import lax
from jax.experimental import pallas as pl
from jax.experimental.pallas import tpu as pltpu

D_MODEL = 1024
W_A = 512
W_B = 512
LRU_BLOCKS = 8
CONV_W = 4
LRU_C = 8.0
H_B = 4
DK = W_B // H_B
N_ADA = 6
PEER_HEADS = 8
N_KEYS = 128
D_QUERY = 256
PEER_TOPK = 16
N_PICK = PEER_HEADS * PEER_TOPK
EPS = 1e-6
CHUNK = 64
N_MAIN = 2 * W_A + 4 * W_B
N_GATE = 2 * H_B

LANES = 128
SUBLANES = 8
PEER_VMEM_LIMIT = 48 * 1024 * 1024

F32 = jnp.float32
BF16 = jnp.bfloat16
HIGHEST = lax.Precision.HIGHEST


def _rms(x, w):
    return x * lax.rsqrt(jnp.mean(x * x, axis=-1, keepdims=True) + EPS) * w


def _adaln_kernel(c_ref, w_ref, b_ref, o_ref):
    s = jax.nn.silu(c_ref[...])
    o_ref[...] = jnp.dot(s, w_ref[...], preferred_element_type=F32, precision=HIGHEST) + b_ref[...]


def _adaln(c, w, b):
    nb = c.shape[0]
    n_out = w.shape[1]
    tn = 1024
    return pl.pallas_call(
        _adaln_kernel,
        grid=(n_out // tn,),
        in_specs=[pl.BlockSpec((nb, D_MODEL), lambda j: (0, 0)),
                  pl.BlockSpec((D_MODEL, tn), lambda j: (0, j)),
                  pl.BlockSpec((1, tn), lambda j: (0, j))],
        out_specs=pl.BlockSpec((nb, tn), lambda j: (0, j)),
        out_shape=jax.ShapeDtypeStruct((nb, n_out), F32),
        name="adaln",
    )(c, w, b.reshape(1, n_out))


def _inproj_kernel(x_ref, mod_ref, nw_ref, w_ref, b_ref, wg_ref, bg_ref, z_ref, zg_ref):
    h = _rms(x_ref[0], nw_ref[...]) * (1.0 + mod_ref[0, 1:2, :]) + mod_ref[0, 0:1, :]
    z_ref[0] = jnp.dot(h.astype(BF16), w_ref[...], preferred_element_type=F32) + b_ref[...]
    zg_ref[0] = jnp.dot(h, wg_ref[...], preferred_element_type=F32, precision=HIGHEST) + bg_ref[...]


def _inproj(x, mod, nw, w_main, b_main, w_gate, b_gate, tm):
    B, T, _ = x.shape
    return pl.pallas_call(
        _inproj_kernel,
        grid=(B, T // tm),
        in_specs=[pl.BlockSpec((1, tm, D_MODEL), lambda b, t: (b, t, 0)),
                  pl.BlockSpec((1, N_ADA, D_MODEL), lambda b, t: (b, 0, 0)),
                  pl.BlockSpec((1, D_MODEL), lambda b, t: (0, 0)),
                  pl.BlockSpec((D_MODEL, N_MAIN), lambda b, t: (0, 0)),
                  pl.BlockSpec((1, N_MAIN), lambda b, t: (0, 0)),
                  pl.BlockSpec((D_MODEL, LANES), lambda b, t: (0, 0)),
                  pl.BlockSpec((1, LANES), lambda b, t: (0, 0))],
        out_specs=[pl.BlockSpec((1, tm, N_MAIN), lambda b, t: (b, t, 0)),
                   pl.BlockSpec((1, tm, LANES), lambda b, t: (b, t, 0))],
        out_shape=[jax.ShapeDtypeStruct((B, T, N_MAIN), F32),
                   jax.ShapeDtypeStruct((B, T, LANES), F32)],
        compiler_params=pltpu.CompilerParams(
            dimension_semantics=("arbitrary", "arbitrary"), vmem_limit_bytes=48 * 1024 * 1024),
        name="inproj",
    )(x, mod, nw, w_main, b_main, w_gate, b_gate)


def _rglru_kernel(z_ref, h0_ref, conv0_ref, cw_ref, cb_ref, wri_ref, bri_ref, lam_ref,
                  ya_ref, hlast_ref, convnew_ref, xbuf, hcar):
    t = pl.program_id(1)
    tt = z_ref.shape[1]

    @pl.when(t == 0)
    def _():
        xbuf[0:SUBLANES, :] = jnp.zeros((SUBLANES, W_A), F32)
        xbuf[SUBLANES - (CONV_W - 1):SUBLANES, :] = conv0_ref[0]
        hcar[...] = h0_ref[0]

    xbuf[SUBLANES:SUBLANES + tt, :] = z_ref[0, :, 0:W_A]
    u = cb_ref[...]
    for k in range(CONV_W):
        off = SUBLANES - (CONV_W - 1) + k
        u = u + xbuf[off:off + tt, :] * cw_ref[k:k + 1, :]
    convnew_ref[0] = xbuf[tt + SUBLANES - (CONV_W - 1):tt + SUBLANES, :]
    xbuf[0:SUBLANES, :] = xbuf[tt:tt + SUBLANES, :]

    ri = jnp.dot(u.astype(BF16), wri_ref[...], preferred_element_type=F32) + bri_ref[...]
    r = jax.nn.sigmoid(ri[:, 0:W_A])
    i = jax.nn.sigmoid(ri[:, W_A:2 * W_A])
    log_a = (-LRU_C) * r * jax.nn.softplus(-lam_ref[...])
    a = jnp.exp(log_a)
    bx = jnp.sqrt(jnp.tanh(-log_a) * (a * a + 1.0)) * (i * u)

    row = lax.broadcasted_iota(jnp.int32, (tt, W_A), 0)
    d = 1
    while d < tt:
        keep = row >= d
        a_sh = jnp.where(keep, pltpu.roll(a, d, 0), 1.0)
        b_sh = jnp.where(keep, pltpu.roll(bx, d, 0), 0.0)
        bx = bx + a * b_sh
        a = a * a_sh
        d *= 2
    h = bx + a * hcar[...]
    hcar[...] = h[tt - 1:tt, :]
    hlast_ref[0] = h[tt - 1:tt, :]
    ya_ref[0] = (h * jax.nn.gelu(z_ref[0, :, W_A:2 * W_A])).astype(ya_ref.dtype)


def _rglru(z, h0, conv0, conv_w, conv_b, w_ri, b_ri, lam, tt):
    B, T, _ = z.shape
    return pl.pallas_call(
        _rglru_kernel,
        grid=(B, T // tt),
        in_specs=[pl.BlockSpec((1, tt, 2 * W_A), lambda b, t: (b, t, 0)),
                  pl.BlockSpec((1, 1, W_A), lambda b, t: (b, 0, 0)),
                  pl.BlockSpec((1, CONV_W - 1, W_A), lambda b, t: (b, 0, 0)),
                  pl.BlockSpec((CONV_W, W_A), lambda b, t: (0, 0)),
                  pl.BlockSpec((1, W_A), lambda b, t: (0, 0)),
                  pl.BlockSpec((W_A, 2 * W_A), lambda b, t: (0, 0)),
                  pl.BlockSpec((1, 2 * W_A), lambda b, t: (0, 0)),
                  pl.BlockSpec((1, W_A), lambda b, t: (0, 0))],
        out_specs=[pl.BlockSpec((1, tt, W_A), lambda b, t: (b, t, 0)),
                   pl.BlockSpec((1, 1, W_A), lambda b, t: (b, 0, 0)),
                   pl.BlockSpec((1, CONV_W - 1, W_A), lambda b, t: (b, 0, 0))],
        out_shape=[jax.ShapeDtypeStruct((B, T, W_A), BF16),
                   jax.ShapeDtypeStruct((B, 1, W_A), F32),
                   jax.ShapeDtypeStruct((B, CONV_W - 1, W_A), F32)],
        scratch_shapes=[pltpu.VMEM((tt + SUBLANES, W_A), F32), pltpu.VMEM((1, W_A), F32)],
        compiler_params=pltpu.CompilerParams(dimension_semantics=("arbitrary", "arbitrary")),
        name="rglru",
    )(z, h0, conv0, conv_w, conv_b, w_ri, b_ri, lam)


def _mlstm_kernel(L, zqk_ref, zvo_ref, zg_ref, c0_ref, n0_ref, m0_ref, nw_ref,
                  yb_ref, cout_ref, nout_ref, mout_ref, c_s, n_s, m_s):
    t = pl.program_id(1)
    nb, tt, _ = zqk_ref.shape

    @pl.when(t == 0)
    def _():
        c_s[...] = c0_ref[...]
        n_s[...] = n0_ref[...]
        m_s[...] = m0_ref[...]

    ri = lax.broadcasted_iota(jnp.int32, (L, L), 0)
    ci = lax.broadcasted_iota(jnp.int32, (L, L), 1)
    causal = ci <= ri
    tri = causal.astype(F32)
    eye = (ci == ri).astype(F32)

    def to_row(col):
        return jnp.sum(col * eye, axis=0, keepdims=True)

    def chunk(c, carry):
        r0 = pl.multiple_of(c * L, L)
        rows = pl.ds(r0, L)
        for bi, hd in [(bi, hd) for bi in range(nb) for hd in range(H_B)]:
            g = zg_ref[bi, rows, :]
            cols = slice(hd * DK, (hd + 1) * DK)
            q = zqk_ref[bi, rows, hd * DK:(hd + 1) * DK] * (DK ** -0.5)
            k = zqk_ref[bi, rows, W_B + hd * DK:W_B + (hd + 1) * DK]
            v = zvo_ref[bi, rows, hd * DK:(hd + 1) * DK]
            o = zvo_ref[bi, rows, W_B + hd * DK:W_B + (hd + 1) * DK]
            ig_c = g[:, hd:hd + 1]
            lf_c = jax.nn.log_sigmoid(g[:, H_B + hd:H_B + hd + 1])
            ig_r = to_row(ig_c)
            b_c = jnp.sum(tri * to_row(lf_c), axis=1, keepdims=True)
            b_r = to_row(b_c)
            m_prev = m_s[bi, hd:hd + 1, 0:1]
            n_prev = n_s[bi, hd:hd + 1, :]
            c_prev = c_s[bi, hd]

            log_d = jnp.where(causal, b_c - b_r + ig_r, -jnp.inf)
            inter = b_c + m_prev
            m_t = jnp.maximum(inter, jnp.max(log_d, axis=1, keepdims=True))
            dmat = jnp.exp(log_d - m_t)
            w_state = jnp.exp(inter - m_t)
            qb = q.astype(BF16)
            s = lax.dot_general(qb, k.astype(BF16), (((1,), (1,)), ((), ())),
                                preferred_element_type=F32) * dmat
            num = (jnp.dot(s.astype(BF16), v.astype(BF16), preferred_element_type=F32)
                   + w_state * jnp.dot(qb, c_prev.astype(BF16), preferred_element_type=F32))
            den = (jnp.sum(s, axis=1, keepdims=True)
                   + w_state * jnp.sum(q * n_prev, axis=1, keepdims=True))
            h = num / jnp.maximum(jnp.abs(den), jnp.exp(-m_t))

            m_new = m_t[L - 1:L, :]
            w_k = jnp.exp(b_c[L - 1:L, :] - b_c + ig_c - m_new)
            w_c = jnp.exp(inter[L - 1:L, :] - m_new)
            kw = k * w_k
            c_s[bi, hd] = w_c * c_prev + jnp.dot(kw.T.astype(BF16), v.astype(BF16),
                                                 preferred_element_type=F32)
            n_s[bi, hd:hd + 1, :] = w_c * n_prev + jnp.sum(kw, axis=0, keepdims=True)
            m_s[bi, hd:hd + 1, :] = jnp.broadcast_to(m_new, (1, LANES))

            hn = h * lax.rsqrt(jnp.mean(h * h, axis=1, keepdims=True) + EPS)
            y = jax.nn.sigmoid(o) * (hn * nw_ref[:, cols])
            yb_ref[bi, rows, hd * DK:(hd + 1) * DK] = y.astype(yb_ref.dtype)
        return carry

    lax.fori_loop(0, tt // L, chunk, 0)
    cout_ref[...] = c_s[...]
    nout_ref[...] = n_s[...]
    mout_ref[...] = m_s[...]


def _mlstm(z, zg, c0, n0, m0, norm_w, L, tt, nb):
    B, T, _ = z.shape
    state_specs = [pl.BlockSpec((nb, H_B, DK, DK), lambda b, t: (b, 0, 0, 0)),
                   pl.BlockSpec((nb, H_B, DK), lambda b, t: (b, 0, 0)),
                   pl.BlockSpec((nb, H_B, LANES), lambda b, t: (b, 0, 0))]
    return pl.pallas_call(
        functools.partial(_mlstm_kernel, L),
        grid=(B // nb, T // tt),
        in_specs=[pl.BlockSpec((nb, tt, 2 * W_B), lambda b, t: (b, t, 1)),
                  pl.BlockSpec((nb, tt, 2 * W_B), lambda b, t: (b, t, 2)),
                  pl.BlockSpec((nb, tt, LANES), lambda b, t: (b, t, 0))]
        + state_specs + [pl.BlockSpec((1, W_B), lambda b, t: (0, 0))],
        out_specs=[pl.BlockSpec((nb, tt, W_B), lambda b, t: (b, t, 0))] + state_specs,
        out_shape=[jax.ShapeDtypeStruct((B, T, W_B), BF16),
                   jax.ShapeDtypeStruct((B, H_B, DK, DK), F32),
                   jax.ShapeDtypeStruct((B, H_B, DK), F32),
                   jax.ShapeDtypeStruct((B, H_B, LANES), F32)],
        scratch_shapes=[pltpu.VMEM((nb, H_B, DK, DK), F32), pltpu.VMEM((nb, H_B, DK), F32),
                        pltpu.VMEM((nb, H_B, LANES), F32)],
        compiler_params=pltpu.CompilerParams(dimension_semantics=("arbitrary", "arbitrary")),
        name="mlstm",
    )(z, z, zg, c0, n0, m0, norm_w)


def _outproj_kernel(x_ref, ya_ref, yb_ref, mod_ref, woa_ref, wob_ref, nw_ref, x1_ref, h2_ref):
    mix = (jnp.dot(ya_ref[0], woa_ref[...], preferred_element_type=F32)
           + jnp.dot(yb_ref[0], wob_ref[...], preferred_element_type=F32))
    x1 = x_ref[0] + mod_ref[0, 2:3, :] * mix
    x1_ref[0] = x1
    h2_ref[0] = _rms(x1, nw_ref[...]) * (1.0 + mod_ref[0, 4:5, :]) + mod_ref[0, 3:4, :]


def _outproj(x, ya, yb, mod, wo_a, wo_b, nw, tm):
    B, T, _ = x.shape
    xspec = pl.BlockSpec((1, tm, D_MODEL), lambda b, t: (b, t, 0))
    yspec = pl.BlockSpec((1, tm, W_A), lambda b, t: (b, t, 0))
    return pl.pallas_call(
        _outproj_kernel,
        grid=(B, T // tm),
        in_specs=[xspec, yspec, yspec,
                  pl.BlockSpec((1, N_ADA, D_MODEL), lambda b, t: (b, 0, 0)),
                  pl.BlockSpec((W_A, D_MODEL), lambda b, t: (0, 0)),
                  pl.BlockSpec((W_B, D_MODEL), lambda b, t: (0, 0)),
                  pl.BlockSpec((1, D_MODEL), lambda b, t: (0, 0))],
        out_specs=[xspec, xspec],
        out_shape=[jax.ShapeDtypeStruct((B, T, D_MODEL), F32)] * 2,
        compiler_params=pltpu.CompilerParams(dimension_semantics=("arbitrary", "arbitrary")),
        name="outproj",
    )(x, ya, yb, mod, wo_a, wo_b, nw)


def _topk_rows(s, ids, k):
    ids = ids.astype(F32)
    big = jnp.float32(2 ** 24)
    slot = lax.broadcasted_iota(jnp.int32, (k, s.shape[1]), 0)
    vals = jnp.zeros((k, s.shape[1]), F32)
    idxs = jnp.zeros((k, s.shape[1]), F32)
    for j in range(k):
        m = jnp.max(s, axis=0, keepdims=True)
        i = jnp.min(jnp.where(s == m, ids, big), axis=0, keepdims=True)
        s = jnp.where(ids == i, -jnp.inf, s)
        vals = jnp.where(slot == j, m, vals)
        idxs = jnp.where(slot == j, i, idxs)
    return vals, idxs.astype(jnp.int32)


def _pair_candidates(v1, v2):
    k, tk = v1.shape
    row = lax.broadcasted_iota(jnp.int32, (SUBLANES, tk), 0)
    sums = [v1[0:1, :] + v2]
    ids = [lax.broadcasted_iota(jnp.int32, (k, tk), 0)]
    for i in range(1, SUBLANES):
        blk = v1[i:i + 1, :] + v2[0:SUBLANES, :]
        sums.append(jnp.where(row < k // (i + 1), blk, -jnp.inf))
        ids.append(row + i * k)
    for i0 in range(SUBLANES, k, SUBLANES):
        sums.append(v1[i0:i0 + SUBLANES, :] + v2[0:1, :])
        ids.append((row + i0) * k)
    return jnp.concatenate(sums, axis=0), jnp.concatenate(ids, axis=0)


def _select_rows(tbl, sel):
    out = jnp.zeros_like(tbl)
    for i in range(tbl.shape[0]):
        out = jnp.where(sel == i, tbl[i:i + 1, :], out)
    return out


def _retrieve_kernel(h2_ref, wq_ref, k1_ref, k2_ref, tix_ref, par_ref, g_ref,
                     tix_scr, par_scr, g_scr):
    q = jnp.dot(h2_ref[...].astype(BF16), wq_ref[...], preferred_element_type=F32)
    half = D_QUERY // 2
    nt = (((1,), (1,)), ((), ()))
    key_ids = lax.broadcasted_iota(jnp.int32, (N_KEYS, q.shape[0]), 0).astype(F32)
    for hd in range(PEER_HEADS):
        qh = q[:, hd * D_QUERY:(hd + 1) * D_QUERY]
        qh = qh * lax.rsqrt(jnp.mean(qh * qh, axis=1, keepdims=True) + EPS)
        s1 = lax.dot_general(k1_ref[...], qh[:, :half], nt, preferred_element_type=F32,
                             precision=HIGHEST)
        s2 = lax.dot_general(k2_ref[...], qh[:, half:], nt, preferred_element_type=F32,
                             precision=HIGHEST)
        v1, i1 = _topk_rows(s1, key_ids, PEER_TOPK)
        v2, i2 = _topk_rows(s2, key_ids, PEER_TOPK)
        cand, cand_ids = _pair_candidates(v1, v2)
        sv, si = _topk_rows(cand, cand_ids, PEER_TOPK)
        e1 = _select_rows(i1, lax.shift_right_logical(si, 4))
        e2 = _select_rows(i2, si & (PEER_TOPK - 1))
        expert = e1 * N_KEYS + e2
        ex = jnp.exp(sv - jnp.max(sv, axis=0, keepdims=True))
        rows = slice(hd * PEER_TOPK, (hd + 1) * PEER_TOPK)
        tix_scr[rows, :] = (lax.shift_right_logical(expert, 1) * SUBLANES).astype(F32)
        par_scr[rows, :] = (expert & 1).astype(F32)
        g_scr[rows, :] = ex / jnp.sum(ex, axis=0, keepdims=True)
    tix_ref[...] = tix_scr[...].T.astype(jnp.int32)
    par_ref[...] = par_scr[...].T.astype(jnp.int32)
    g_ref[...] = g_scr[...].T


def _retrieve(h2, wq, k1, k2, tk):
    n = h2.shape[0]
    ospec = pl.BlockSpec((tk, N_PICK), lambda i: (i, 0))
    return pl.pallas_call(
        _retrieve_kernel,
        grid=(n // tk,),
        in_specs=[pl.BlockSpec((tk, D_MODEL), lambda i: (i, 0)),
                  pl.BlockSpec((D_MODEL, PEER_HEADS * D_QUERY), lambda i: (0, 0)),
                  pl.BlockSpec((N_KEYS, D_QUERY // 2), lambda i: (0, 0)),
                  pl.BlockSpec((N_KEYS, D_QUERY // 2), lambda i: (0, 0))],
        out_specs=[ospec, ospec, ospec],
        out_shape=[jax.ShapeDtypeStruct((n, N_PICK), jnp.int32),
                   jax.ShapeDtypeStruct((n, N_PICK), jnp.int32),
                   jax.ShapeDtypeStruct((n, N_PICK), F32)],
        scratch_shapes=[pltpu.VMEM((N_PICK, tk), F32)] * 3,
        compiler_params=pltpu.CompilerParams(dimension_semantics=("arbitrary",)),
        name="retrieve",
    )(h2, wq, k1, k2)


TILE_ROWS = 2 * SUBLANES
IDX_PARTS = 2
KG = N_PICK * TILE_ROWS


def _tile_table(tbl):
    halves = lax.bitcast_convert_type(tbl.astype(BF16), jnp.uint16).astype(jnp.uint32)
    halves = halves.reshape(-1, SUBLANES, 2, LANES)
    words = halves[:, :, 0, :] | (halves[:, :, 1, :] << 16)
    return words.reshape(-1, LANES)


def _gathered_rows(tbl_ref, off_ref, p):
    tiles = []
    for j in range(N_PICK):
        off = pl.multiple_of(off_ref[p, j], SUBLANES)
        tiles.append(pltpu.bitcast(tbl_ref[pl.ds(off, SUBLANES), :], BF16))
    return jnp.concatenate(tiles, axis=0)


def _for_each_token(off_hbm, off_smem, sems, tb, one_token):
    step = pl.program_id(0)
    part = tb // IDX_PARTS

    def copy(block, k):
        rows = pl.ds(k * part, part)
        return pltpu.make_async_copy(off_hbm.at[pl.ds(block * tb + k * part, part)],
                                     off_smem.at[rows], sems.at[k])

    @pl.when(step == 0)
    def _():
        for k in range(IDX_PARTS):
            copy(0, k).start()

    for k in range(IDX_PARTS):
        copy(step, k).wait()
        for p in range(k * part, (k + 1) * part):
            one_token(p)

        @pl.when(step + 1 < pl.num_programs(0))
        def _():
            copy(step + 1, k).start()


def _split_bf16(x):
    hi = x.astype(BF16)
    return hi, (x - hi.astype(F32)).astype(BF16)


def _pick_patterns():
    col = jnp.arange(KG, dtype=jnp.int32)
    pick = jnp.arange(N_PICK, dtype=jnp.int32)
    expand = (col[None, :] // TILE_ROWS == pick[:, None]).astype(BF16)
    half = col[:, None] // SUBLANES
    sum_even = (half == 2 * pick[None, :]).astype(BF16)
    sum_odd = (half == 2 * pick[None, :] + 1).astype(BF16)
    return expand, sum_even, sum_odd


N_MXU = N_PICK // 2
KM = N_MXU * TILE_ROWS


def _peer_act_kernel(tix_hbm, x_ref, par_ref, g_ref, sum_even_ref, sum_odd_ref, tbl_ref, w_ref,
                     xa_scr, zs_scr, r_scr, actv_scr, tix_smem, sems):
    tb = x_ref.shape[0]
    x = x_ref[...]
    xa_scr[...] = jnp.concatenate([x, x - x.astype(BF16).astype(F32)], axis=1).astype(BF16)
    r_scr[...] = jnp.zeros_like(r_scr)
    keep = ((lax.broadcasted_iota(jnp.int32, (SUBLANES, KM), 1) & (SUBLANES - 1))
            == lax.broadcasted_iota(jnp.int32, (SUBLANES, KM), 0))
    nt = (((1,), (1,)), ((), ()))
    half = SUBLANES // 2

    def one_token(p):
        tiles = []
        for j in range(N_MXU):
            off = pl.multiple_of(tix_smem[p, j], SUBLANES)
            tiles.append(pltpu.bitcast(tbl_ref[pl.ds(off, SUBLANES), :], BF16))
        rows = jnp.concatenate(tiles, axis=0)
        res = lax.dot_general(xa_scr[p], rows, nt, preferred_element_type=F32)
        z = jnp.where(keep, res[0:SUBLANES] + res[SUBLANES:], 0.0)
        zs_scr[p:p + 1, :] = jnp.sum(z, axis=0, keepdims=True)

        xp = x_ref[p]
        x_even = jnp.concatenate([xp[c:c + 1] for c in (0, 2, 4, 6)] * 2, axis=0)
        x_odd = jnp.concatenate([xp[c:c + 1] for c in (1, 3, 5, 7)] * 2, axis=0)
        for j in range(N_MXU, N_PICK):
            off = pl.multiple_of(tix_smem[p, j], SUBLANES)
            word = tbl_ref[pl.ds(off, SUBLANES), :]
            rows_even = lax.bitcast_convert_type(word << 16, F32)
            rows_odd = lax.bitcast_convert_type(word & jnp.uint32(0xFFFF0000), F32)
            r_scr[:, j:j + 1] = jnp.sum(rows_even * x_even + rows_odd * x_odd, axis=1, keepdims=True)
        r = r_scr[...]
        first = jnp.sum(r[0:half], axis=0, keepdims=True)
        second = jnp.sum(r[half:], axis=0, keepdims=True)
        actv_scr[p:p + 1, :] = jnp.where(par_ref[p:p + 1, :] == 1, second, first)

    _for_each_token(tix_hbm, tix_smem, sems, tb, one_token)

    sum_even = sum_even_ref[0:KM, :]
    sum_odd = sum_odd_ref[0:KM, :]
    z_hi, z_lo = _split_bf16(zs_scr[...])
    even = (jnp.dot(z_hi, sum_even, preferred_element_type=F32)
            + jnp.dot(z_lo, sum_even, preferred_element_type=F32))
    odd = (jnp.dot(z_hi, sum_odd, preferred_element_type=F32)
           + jnp.dot(z_lo, sum_odd, preferred_element_type=F32))
    act = jnp.where(par_ref[...] == 1, odd, even) + actv_scr[...]
    w_ref[...] = g_ref[...] * jax.nn.gelu(act)


def _peer_act(tix, x3, par, gates, sum_even, sum_odd, tbl, tb):
    n = tix.shape[0]
    vspec = pl.BlockSpec((tb, N_PICK), lambda i: (i, 0))
    whole = pl.BlockSpec(memory_space=pltpu.VMEM)
    return pl.pallas_call(
        _peer_act_kernel,
        grid=(n // tb,),
        in_specs=[pl.BlockSpec(memory_space=pl.ANY),
                  pl.BlockSpec((tb, SUBLANES, LANES), lambda i: (i, 0, 0)),
                  vspec, vspec, whole, whole, whole],
        out_specs=vspec,
        out_shape=jax.ShapeDtypeStruct((n, N_PICK), F32),
        scratch_shapes=[pltpu.VMEM((tb, TILE_ROWS, LANES), BF16), pltpu.VMEM((tb, KM), F32),
                        pltpu.VMEM((SUBLANES, N_PICK), F32), pltpu.VMEM((tb, N_PICK), F32),
                        pltpu.SMEM((tb, N_PICK), jnp.int32),
                        pltpu.SemaphoreType.DMA((IDX_PARTS,))],
        compiler_params=pltpu.CompilerParams(
            dimension_semantics=("arbitrary",), vmem_limit_bytes=PEER_VMEM_LIMIT),
        name="peer_act",
    )(tix, x3, par, gates, sum_even, sum_odd, tbl)


def _peer_out_kernel(tix_hbm, w_ref, par_ref, x1_ref, g2_ref, fw_ref, expand_ref, tbl_ref, y_ref,
                     wexp_scr, sum_scr, tix_smem, sems):
    tb = x1_ref.shape[0]

    expand = expand_ref[...]
    w_hi, w_lo = _split_bf16(w_ref[...])
    par8 = (par_ref[...] * SUBLANES).astype(F32).astype(BF16)
    wexp_scr[0:tb, :] = jnp.dot(w_hi, expand, preferred_element_type=F32)
    wexp_scr[tb:2 * tb, :] = jnp.dot(w_lo, expand, preferred_element_type=F32)
    wexp_scr[2 * tb:3 * tb, :] = jnp.dot(par8, expand, preferred_element_type=F32)

    rel = ((lax.broadcasted_iota(jnp.int32, (SUBLANES, KG), 1) & (TILE_ROWS - 1))
           - lax.broadcasted_iota(jnp.int32, (SUBLANES, KG), 0)).astype(F32)

    def one_token(p):
        sel = rel == wexp_scr[2 * tb + p:2 * tb + p + 1, :]
        a_hi = jnp.where(sel, wexp_scr[p:p + 1, :], 0.0)
        a_lo = jnp.where(sel, wexp_scr[tb + p:tb + p + 1, :], 0.0)
        a = jnp.concatenate([a_hi, a_lo], axis=0).astype(BF16)
        rows = _gathered_rows(tbl_ref, tix_smem, p)
        res = jnp.dot(a, rows, preferred_element_type=F32)
        sum_scr[p] = res[0:SUBLANES] + res[SUBLANES:]

    _for_each_token(tix_hbm, tix_smem, sems, tb, one_token)

    out = jnp.concatenate([sum_scr[:, c, :] for c in range(SUBLANES)], axis=1)
    y_ref[...] = _rms(x1_ref[...] + g2_ref[0] * out, fw_ref[...])


def _peer_out(tix, w, par, x1, g2, fw, expand, tbl, tb, tokens_per_batch):
    n = tix.shape[0]
    vspec = pl.BlockSpec((tb, N_PICK), lambda i: (i, 0))
    xspec = pl.BlockSpec((tb, D_MODEL), lambda i: (i, 0))
    whole = pl.BlockSpec(memory_space=pltpu.VMEM)
    return pl.pallas_call(
        _peer_out_kernel,
        grid=(n // tb,),
        in_specs=[pl.BlockSpec(memory_space=pl.ANY),
                  vspec, vspec, xspec,
                  pl.BlockSpec((1, 1, D_MODEL), lambda i: ((i * tb) // tokens_per_batch, 0, 0)),
                  pl.BlockSpec((1, D_MODEL), lambda i: (0, 0)),
                  whole, whole],
        out_specs=xspec,
        out_shape=jax.ShapeDtypeStruct((n, D_MODEL), F32),
        scratch_shapes=[pltpu.VMEM((3 * tb, KG), F32),
                        pltpu.VMEM((tb, SUBLANES, LANES), F32),
                        pltpu.SMEM((tb, N_PICK), jnp.int32),
                        pltpu.SemaphoreType.DMA((IDX_PARTS,))],
        compiler_params=pltpu.CompilerParams(
            dimension_semantics=("arbitrary",), vmem_limit_bytes=PEER_VMEM_LIMIT),
        name="peer_out",
    )(tix, w, par, x1, g2, fw, expand, tbl)


def _trunk(x, mod, h0, conv0, c0, n0, m0, chunk_len, p):
    B, T, _ = x.shape
    n = B * T
    tm = min(512, T)
    z, zg = _inproj(x, mod, p["norm1_w"], p["w_main"], p["b_main"], p["w_gate"], p["b_gate"], tm)
    ya, h_last, conv_new = _rglru(z, h0.reshape(B, 1, W_A), conv0, p["conv_w"], p["conv_b"],
                                  p["w_ri"], p["b_ri"], p["lam"], tm)
    m0b = jnp.broadcast_to(m0[:, :, None], (B, H_B, LANES))
    yb, c_new, n_new, m_new = _mlstm(z, zg, c0, n0, m0b, p["mlstm_norm_w"], chunk_len, tm, 1)
    x1, h2 = _outproj(x, ya, yb, mod, p["wo_a"], p["wo_b"], p["norm2_w"], tm)

    tk = 256
    tix, par, gates = _retrieve(h2.reshape(n, D_MODEL), p["wq"], p["k1"], p["k2"], tk)
    tb = min(64, T)
    expand, sum_even, sum_odd = _pick_patterns()
    w = _peer_act(tix, h2.reshape(n, SUBLANES, LANES), par, gates, sum_even, sum_odd,
                  p["u_tbl"], tb)
    y = _peer_out(tix, w, par, x1.reshape(n, D_MODEL), mod[:, 5:6], p["final_w"], expand,
                  p["v_tbl"], tb, T)
    return (y.reshape(B, T, D_MODEL), h_last.reshape(B, W_A), conv_new, c_new, n_new,
            m_new[:, :, 0])


def kernel(x_prompt, x_sample, c_prompt, c_sample, state_lru_h, state_lru_conv, state_mlstm_C,
           state_mlstm_n, state_mlstm_m, norm1_w, ada_w, ada_b, w_in, b_in, conv_w, conv_b,
           lru_wr, lru_br, lru_wi, lru_bi, lru_lambda, mlstm_norm_w, w_out, norm2_w,
           peer_wq, peer_k1, peer_k2, peer_u, peer_v, final_norm_w):
    depth = w_in.shape[0]
    assert depth == 1, "single-layer trunk only"
    B = x_prompt.shape[0]
    Bs = x_sample.shape[0]
    l = 0

    def block_diag(w):
        nb, c, d = w.shape
        eye = jnp.eye(nb, dtype=w.dtype)
        return jnp.einsum("ncd,nm->ncmd", w, eye).reshape(nb * c, nb * d)

    pad_g = LANES - N_GATE
    p = {
        "norm1_w": norm1_w[l].reshape(1, D_MODEL),
        "w_main": w_in[l][:, :N_MAIN].astype(BF16),
        "b_main": b_in[l][:N_MAIN].reshape(1, N_MAIN),
        "w_gate": jnp.pad(w_in[l][:, N_MAIN:], ((0, 0), (0, pad_g))),
        "b_gate": jnp.pad(b_in[l][N_MAIN:], (0, pad_g)).reshape(1, LANES),
        "conv_w": conv_w[l],
        "conv_b": conv_b[l].reshape(1, W_A),
        "w_ri": jnp.concatenate([block_diag(lru_wr[l]), block_diag(lru_wi[l])], axis=1).astype(BF16),
        "b_ri": jnp.concatenate([lru_br[l], lru_bi[l]]).reshape(1, 2 * W_A),
        "lam": lru_lambda[l].reshape(1, W_A),
        "mlstm_norm_w": mlstm_norm_w[l].reshape(1, W_B),
        "wo_a": w_out[l][:W_A].astype(BF16),
        "wo_b": w_out[l][W_A:].astype(BF16),
        "norm2_w": norm2_w[l].reshape(1, D_MODEL),
        "wq": peer_wq[l].astype(BF16),
        "k1": peer_k1[l],
        "k2": peer_k2[l],
        "u_tbl": _tile_table(peer_u[l]),
        "v_tbl": _tile_table(peer_v[l]),
        "final_w": final_norm_w.reshape(1, D_MODEL),
    }

    mod = _adaln(jnp.concatenate([c_prompt, c_sample], axis=0), ada_w[l], ada_b[l])
    mod = mod.reshape(B + Bs, N_ADA, D_MODEL)

    dt = x_prompt.dtype
    outs_p = _trunk(x_prompt, mod[:B], jnp.zeros((B, W_A), dt), jnp.zeros((B, CONV_W - 1, W_A), dt),
                    jnp.zeros((B, H_B, DK, DK), dt), jnp.zeros((B, H_B, DK), dt),
                    jnp.zeros((B, H_B), dt), CHUNK, p)
    outs_s = _trunk(x_sample, mod[B:], state_lru_h[l], state_lru_conv[l], state_mlstm_C[l],
                    state_mlstm_n[l], state_mlstm_m[l], x_sample.shape[1], p)
    y_p, *st_p = outs_p
    y_s, *st_s = outs_s
    st_p = [s[None].astype(r.dtype) for s, r in zip(
        st_p, (state_lru_h, state_lru_conv, state_mlstm_C, state_mlstm_n, state_mlstm_m))]
    st_s = [s[None].astype(r.dtype) for s, r in zip(
        st_s, (state_lru_h, state_lru_conv, state_mlstm_C, state_mlstm_n, state_mlstm_m))]
    return (y_p, y_s, *st_p, *st_s)
```

```python
import functools

import jax
import jax.numpy as jnp
from jax import lax
from jax.experimental import pallas as pl
from jax.experimental.pallas import tpu as pltpu

D_MODEL = 1024
W_A = 512
W_B = 512
LRU_BLOCKS = 8
CONV_W = 4
LRU_C = 8.0
H_B = 4
DK = W_B // H_B
N_ADA = 6
PEER_HEADS = 8
N_KEYS = 128
D_QUERY = 256
PEER_TOPK = 16
N_PICK = PEER_HEADS * PEER_TOPK
EPS = 1e-6
CHUNK = 64
N_MAIN = 2 * W_A + 4 * W_B
N_GATE = 2 * H_B

LANES = 128
SUBLANES = 8
PEER_VMEM_LIMIT = 48 * 1024 * 1024

F32 = jnp.float32
BF16 = jnp.bfloat16
HIGHEST = lax.Precision.HIGHEST


def _rms(x, w):
    return x * lax.rsqrt(jnp.mean(x * x, axis=-1, keepdims=True) + EPS) * w


def _adaln_kernel(c_ref, w_ref, b_ref, o_ref):
    s = jax.nn.silu(c_ref[...])
    o_ref[...] = jnp.dot(s, w_ref[...], preferred_element_type=F32, precision=HIGHEST) + b_ref[...]


def _adaln(c, w, b):
    nb = c.shape[0]
    n_out = w.shape[1]
    tn = 1024
    return pl.pallas_call(
        _adaln_kernel,
        grid=(n_out // tn,),
        in_specs=[pl.BlockSpec((nb, D_MODEL), lambda j: (0, 0)),
                  pl.BlockSpec((D_MODEL, tn), lambda j: (0, j)),
                  pl.BlockSpec((1, tn), lambda j: (0, j))],
        out_specs=pl.BlockSpec((nb, tn), lambda j: (0, j)),
        out_shape=jax.ShapeDtypeStruct((nb, n_out), F32),
        name="adaln",
    )(c, w, b.reshape(1, n_out))


def _inproj_kernel(x_ref, mod_ref, nw_ref, w_ref, b_ref, wg_ref, bg_ref, z_ref, zg_ref):
    h = _rms(x_ref[0], nw_ref[...]) * (1.0 + mod_ref[0, 1:2, :]) + mod_ref[0, 0:1, :]
    z_ref[0] = jnp.dot(h.astype(BF16), w_ref[...], preferred_element_type=F32) + b_ref[...]
    zg_ref[0] = jnp.dot(h, wg_ref[...], preferred_element_type=F32, precision=HIGHEST) + bg_ref[...]


def _inproj(x, mod, nw, w_main, b_main, w_gate, b_gate, tm):
    B, T, _ = x.shape
    return pl.pallas_call(
        _inproj_kernel,
        grid=(B, T // tm),
        in_specs=[pl.BlockSpec((1, tm, D_MODEL), lambda b, t: (b, t, 0)),
                  pl.BlockSpec((1, N_ADA, D_MODEL), lambda b, t: (b, 0, 0)),
                  pl.BlockSpec((1, D_MODEL), lambda b, t: (0, 0)),
                  pl.BlockSpec((D_MODEL, N_MAIN), lambda b, t: (0, 0)),
                  pl.BlockSpec((1, N_MAIN), lambda b, t: (0, 0)),
                  pl.BlockSpec((D_MODEL, LANES), lambda b, t: (0, 0)),
                  pl.BlockSpec((1, LANES), lambda b, t: (0, 0))],
        out_specs=[pl.BlockSpec((1, tm, N_MAIN), lambda b, t: (b, t, 0)),
                   pl.BlockSpec((1, tm, LANES), lambda b, t: (b, t, 0))],
        out_shape=[jax.ShapeDtypeStruct((B, T, N_MAIN), F32),
                   jax.ShapeDtypeStruct((B, T, LANES), F32)],
        compiler_params=pltpu.CompilerParams(
            dimension_semantics=("arbitrary", "arbitrary"), vmem_limit_bytes=48 * 1024 * 1024),
        name="inproj",
    )(x, mod, nw, w_main, b_main, w_gate, b_gate)


def _rglru_kernel(z_ref, h0_ref, conv0_ref, cw_ref, cb_ref, wri_ref, bri_ref, lam_ref,
                  ya_ref, hlast_ref, convnew_ref, xbuf, hcar):
    t = pl.program_id(1)
    tt = z_ref.shape[1]

    @pl.when(t == 0)
    def _():
        xbuf[0:SUBLANES, :] = jnp.zeros((SUBLANES, W_A), F32)
        xbuf[SUBLANES - (CONV_W - 1):SUBLANES, :] = conv0_ref[0]
        hcar[...] = h0_ref[0]

    xbuf[SUBLANES:SUBLANES + tt, :] = z_ref[0, :, 0:W_A]
    u = cb_ref[...]
    for k in range(CONV_W):
        off = SUBLANES - (CONV_W - 1) + k
        u = u + xbuf[off:off + tt, :] * cw_ref[k:k + 1, :]
    convnew_ref[0] = xbuf[tt + SUBLANES - (CONV_W - 1):tt + SUBLANES, :]
    xbuf[0:SUBLANES, :] = xbuf[tt:tt + SUBLANES, :]

    ri = jnp.dot(u.astype(BF16), wri_ref[...], preferred_element_type=F32) + bri_ref[...]
    r = jax.nn.sigmoid(ri[:, 0:W_A])
    i = jax.nn.sigmoid(ri[:, W_A:2 * W_A])
    log_a = (-LRU_C) * r * jax.nn.softplus(-lam_ref[...])
    a = jnp.exp(log_a)
    bx = jnp.sqrt(jnp.tanh(-log_a) * (a * a + 1.0)) * (i * u)

    row = lax.broadcasted_iota(jnp.int32, (tt, W_A), 0)
    d = 1
    while d < tt:
        keep = row >= d
        a_sh = jnp.where(keep, pltpu.roll(a, d, 0), 1.0)
        b_sh = jnp.where(keep, pltpu.roll(bx, d, 0), 0.0)
        bx = bx + a * b_sh
        a = a * a_sh
        d *= 2
    h = bx + a * hcar[...]
    hcar[...] = h[tt - 1:tt, :]
    hlast_ref[0] = h[tt - 1:tt, :]
    ya_ref[0] = (h * jax.nn.gelu(z_ref[0, :, W_A:2 * W_A])).astype(ya_ref.dtype)


def _rglru(z, h0, conv0, conv_w, conv_b, w_ri, b_ri, lam, tt):
    B, T, _ = z.shape
    return pl.pallas_call(
        _rglru_kernel,
        grid=(B, T // tt),
        in_specs=[pl.BlockSpec((1, tt, 2 * W_A), lambda b, t: (b, t, 0)),
                  pl.BlockSpec((1, 1, W_A), lambda b, t: (b, 0, 0)),
                  pl.BlockSpec((1, CONV_W - 1, W_A), lambda b, t: (b, 0, 0)),
                  pl.BlockSpec((CONV_W, W_A), lambda b, t: (0, 0)),
                  pl.BlockSpec((1, W_A), lambda b, t: (0, 0)),
                  pl.BlockSpec((W_A, 2 * W_A), lambda b, t: (0, 0)),
                  pl.BlockSpec((1, 2 * W_A), lambda b, t: (0, 0)),
                  pl.BlockSpec((1, W_A), lambda b, t: (0, 0))],
        out_specs=[pl.BlockSpec((1, tt, W_A), lambda b, t: (b, t, 0)),
                   pl.BlockSpec((1, 1, W_A), lambda b, t: (b, 0, 0)),
                   pl.BlockSpec((1, CONV_W - 1, W_A), lambda b, t: (b, 0, 0))],
        out_shape=[jax.ShapeDtypeStruct((B, T, W_A), BF16),
                   jax.ShapeDtypeStruct((B, 1, W_A), F32),
                   jax.ShapeDtypeStruct((B, CONV_W - 1, W_A), F32)],
        scratch_shapes=[pltpu.VMEM((tt + SUBLANES, W_A), F32), pltpu.VMEM((1, W_A), F32)],
        compiler_params=pltpu.CompilerParams(dimension_semantics=("arbitrary", "arbitrary")),
        name="rglru",
    )(z, h0, conv0, conv_w, conv_b, w_ri, b_ri, lam)


def _mlstm_kernel(L, zqk_ref, zvo_ref, zg_ref, c0_ref, n0_ref, m0_ref, nw_ref,
                  yb_ref, cout_ref, nout_ref, mout_ref, c_s, n_s, m_s):
    t = pl.program_id(1)
    nb, tt, _ = zqk_ref.shape

    @pl.when(t == 0)
    def _():
        c_s[...] = c0_ref[...]
        n_s[...] = n0_ref[...]
        m_s[...] = m0_ref[...]

    ri = lax.broadcasted_iota(jnp.int32, (L, L), 0)
    ci = lax.broadcasted_iota(jnp.int32, (L, L), 1)
    causal = ci <= ri
    tri = causal.astype(F32)
    eye = (ci == ri).astype(F32)

    def to_row(col):
        return jnp.sum(col * eye, axis=0, keepdims=True)

    def chunk(c, carry):
        r0 = pl.multiple_of(c * L, L)
        rows = pl.ds(r0, L)
        for bi, hd in [(bi, hd) for bi in range(nb) for hd in range(H_B)]:
            g = zg_ref[bi, rows, :]
            cols = slice(hd * DK, (hd + 1) * DK)
            q = zqk_ref[bi, rows, hd * DK:(hd + 1) * DK] * (DK ** -0.5)
            k = zqk_ref[bi, rows, W_B + hd * DK:W_B + (hd + 1) * DK]
            v = zvo_ref[bi, rows, hd * DK:(hd + 1) * DK]
            o = zvo_ref[bi, rows, W_B + hd * DK:W_B + (hd + 1) * DK]
            ig_c = g[:, hd:hd + 1]
            lf_c = jax.nn.log_sigmoid(g[:, H_B + hd:H_B + hd + 1])
            ig_r = to_row(ig_c)
            b_c = jnp.sum(tri * to_row(lf_c), axis=1, keepdims=True)
            b_r = to_row(b_c)
            m_prev = m_s[bi, hd:hd + 1, 0:1]
            n_prev = n_s[bi, hd:hd + 1, :]
            c_prev = c_s[bi, hd]

            log_d = jnp.where(causal, b_c - b_r + ig_r, -jnp.inf)
            inter = b_c + m_prev
            m_t = jnp.maximum(inter, jnp.max(log_d, axis=1, keepdims=True))
            dmat = jnp.exp(log_d - m_t)
            w_state = jnp.exp(inter - m_t)
            qb = q.astype(BF16)
            s = lax.dot_general(qb, k.astype(BF16), (((1,), (1,)), ((), ())),
                                preferred_element_type=F32) * dmat
            num = (jnp.dot(s.astype(BF16), v.astype(BF16), preferred_element_type=F32)
                   + w_state * jnp.dot(qb, c_prev.astype(BF16), preferred_element_type=F32))
            den = (jnp.sum(s, axis=1, keepdims=True)
                   + w_state * jnp.sum(q * n_prev, axis=1, keepdims=True))
            h = num / jnp.maximum(jnp.abs(den), jnp.exp(-m_t))

            m_new = m_t[L - 1:L, :]
            w_k = jnp.exp(b_c[L - 1:L, :] - b_c + ig_c - m_new)
            w_c = jnp.exp(inter[L - 1:L, :] - m_new)
            kw = k * w_k
            c_s[bi, hd] = w_c * c_prev + jnp.dot(kw.T.astype(BF16), v.astype(BF16),
                                                 preferred_element_type=F32)
            n_s[bi, hd:hd + 1, :] = w_c * n_prev + jnp.sum(kw, axis=0, keepdims=True)
            m_s[bi, hd:hd + 1, :] = jnp.broadcast_to(m_new, (1, LANES))

            hn = h * lax.rsqrt(jnp.mean(h * h, axis=1, keepdims=True) + EPS)
            y = jax.nn.sigmoid(o) * (hn * nw_ref[:, cols])
            yb_ref[bi, rows, hd * DK:(hd + 1) * DK] = y.astype(yb_ref.dtype)
        return carry

    lax.fori_loop(0, tt // L, chunk, 0)
    cout_ref[...] = c_s[...]
    nout_ref[...] = n_s[...]
    mout_ref[...] = m_s[...]


def _mlstm(z, zg, c0, n0, m0, norm_w, L, tt, nb):
    B, T, _ = z.shape
    state_specs = [pl.BlockSpec((nb, H_B, DK, DK), lambda b, t: (b, 0, 0, 0)),
                   pl.BlockSpec((nb, H_B, DK), lambda b, t: (b, 0, 0)),
                   pl.BlockSpec((nb, H_B, LANES), lambda b, t: (b, 0, 0))]
    return pl.pallas_call(
        functools.partial(_mlstm_kernel, L),
        grid=(B // nb, T // tt),
        in_specs=[pl.BlockSpec((nb, tt, 2 * W_B), lambda b, t: (b, t, 1)),
                  pl.BlockSpec((nb, tt, 2 * W_B), lambda b, t: (b, t, 2)),
                  pl.BlockSpec((nb, tt, LANES), lambda b, t: (b, t, 0))]
        + state_specs + [pl.BlockSpec((1, W_B), lambda b, t: (0, 0))],
        out_specs=[pl.BlockSpec((nb, tt, W_B), lambda b, t: (b, t, 0))] + state_specs,
        out_shape=[jax.ShapeDtypeStruct((B, T, W_B), BF16),
                   jax.ShapeDtypeStruct((B, H_B, DK, DK), F32),
                   jax.ShapeDtypeStruct((B, H_B, DK), F32),
                   jax.ShapeDtypeStruct((B, H_B, LANES), F32)],
        scratch_shapes=[pltpu.VMEM((nb, H_B, DK, DK), F32), pltpu.VMEM((nb, H_B, DK), F32),
                        pltpu.VMEM((nb, H_B, LANES), F32)],
        compiler_params=pltpu.CompilerParams(dimension_semantics=("arbitrary", "arbitrary")),
        name="mlstm",
    )(z, z, zg, c0, n0, m0, norm_w)


def _outproj_kernel(x_ref, ya_ref, yb_ref, mod_ref, woa_ref, wob_ref, nw_ref, x1_ref, h2_ref):
    mix = (jnp.dot(ya_ref[0], woa_ref[...], preferred_element_type=F32)
           + jnp.dot(yb_ref[0], wob_ref[...], preferred_element_type=F32))
    x1 = x_ref[0] + mod_ref[0, 2:3, :] * mix
    x1_ref[0] = x1
    h2_ref[0] = _rms(x1, nw_ref[...]) * (1.0 + mod_ref[0, 4:5, :]) + mod_ref[0, 3:4, :]


def _outproj(x, ya, yb, mod, wo_a, wo_b, nw, tm):
    B, T, _ = x.shape
    xspec = pl.BlockSpec((1, tm, D_MODEL), lambda b, t: (b, t, 0))
    yspec = pl.BlockSpec((1, tm, W_A), lambda b, t: (b, t, 0))
    return pl.pallas_call(
        _outproj_kernel,
        grid=(B, T // tm),
        in_specs=[xspec, yspec, yspec,
                  pl.BlockSpec((1, N_ADA, D_MODEL), lambda b, t: (b, 0, 0)),
                  pl.BlockSpec((W_A, D_MODEL), lambda b, t: (0, 0)),
                  pl.BlockSpec((W_B, D_MODEL), lambda b, t: (0, 0)),
                  pl.BlockSpec((1, D_MODEL), lambda b, t: (0, 0))],
        out_specs=[xspec, xspec],
        out_shape=[jax.ShapeDtypeStruct((B, T, D_MODEL), F32)] * 2,
        compiler_params=pltpu.CompilerParams(dimension_semantics=("arbitrary", "arbitrary")),
        name="outproj",
    )(x, ya, yb, mod, wo_a, wo_b, nw)


def _topk_rows(s, ids, k):
    ids = ids.astype(F32)
    big = jnp.float32(2 ** 24)
    slot = lax.broadcasted_iota(jnp.int32, (k, s.shape[1]), 0)
    vals = jnp.zeros((k, s.shape[1]), F32)
    idxs = jnp.zeros((k, s.shape[1]), F32)
    for j in range(k):
        m = jnp.max(s, axis=0, keepdims=True)
        i = jnp.min(jnp.where(s == m, ids, big), axis=0, keepdims=True)
        s = jnp.where(ids == i, -jnp.inf, s)
        vals = jnp.where(slot == j, m, vals)
        idxs = jnp.where(slot == j, i, idxs)
    return vals, idxs.astype(jnp.int32)


def _pair_candidates(v1, v2):
    k, tk = v1.shape
    row = lax.broadcasted_iota(jnp.int32, (SUBLANES, tk), 0)
    sums = [v1[0:1, :] + v2]
    ids = [lax.broadcasted_iota(jnp.int32, (k, tk), 0)]
    for i in range(1, SUBLANES):
        blk = v1[i:i + 1, :] + v2[0:SUBLANES, :]
        sums.append(jnp.where(row < k // (i + 1), blk, -jnp.inf))
        ids.append(row + i * k)
    for i0 in range(SUBLANES, k, SUBLANES):
        sums.append(v1[i0:i0 + SUBLANES, :] + v2[0:1, :])
        ids.append((row + i0) * k)
    return jnp.concatenate(sums, axis=0), jnp.concatenate(ids, axis=0)


def _select_rows(tbl, sel):
    out = jnp.zeros_like(tbl)
    for i in range(tbl.shape[0]):
        out = jnp.where(sel == i, tbl[i:i + 1, :], out)
    return out


def _retrieve_kernel(h2_ref, wq_ref, k1_ref, k2_ref, tix_ref, par_ref, g_ref,
                     tix_scr, par_scr, g_scr):
    q = jnp.dot(h2_ref[...].astype(BF16), wq_ref[...], preferred_element_type=F32)
    half = D_QUERY // 2
    nt = (((1,), (1,)), ((), ()))
    key_ids = lax.broadcasted_iota(jnp.int32, (N_KEYS, q.shape[0]), 0).astype(F32)
    for hd in range(PEER_HEADS):
        qh = q[:, hd * D_QUERY:(hd + 1) * D_QUERY]
        qh = qh * lax.rsqrt(jnp.mean(qh * qh, axis=1, keepdims=True) + EPS)
        s1 = lax.dot_general(k1_ref[...], qh[:, :half], nt, preferred_element_type=F32,
                             precision=HIGHEST)
        s2 = lax.dot_general(k2_ref[...], qh[:, half:], nt, preferred_element_type=F32,
                             precision=HIGHEST)
        v1, i1 = _topk_rows(s1, key_ids, PEER_TOPK)
        v2, i2 = _topk_rows(s2, key_ids, PEER_TOPK)
        cand, cand_ids = _pair_candidates(v1, v2)
        sv, si = _topk_rows(cand, cand_ids, PEER_TOPK)
        e1 = _select_rows(i1, lax.shift_right_logical(si, 4))
        e2 = _select_rows(i2, si & (PEER_TOPK - 1))
        expert = e1 * N_KEYS + e2
        ex = jnp.exp(sv - jnp.max(sv, axis=0, keepdims=True))
        rows = slice(hd * PEER_TOPK, (hd + 1) * PEER_TOPK)
        tix_scr[rows, :] = (lax.shift_right_logical(expert, 1) * SUBLANES).astype(F32)
        par_scr[rows, :] = (expert & 1).astype(F32)
        g_scr[rows, :] = ex / jnp.sum(ex, axis=0, keepdims=True)
    tix_ref[...] = tix_scr[...].T.astype(jnp.int32)
    par_ref[...] = par_scr[...].T.astype(jnp.int32)
    g_ref[...] = g_scr[...].T


def _retrieve(h2, wq, k1, k2, tk):
    n = h2.shape[0]
    ospec = pl.BlockSpec((tk, N_PICK), lambda i: (i, 0))
    return pl.pallas_call(
        _retrieve_kernel,
        grid=(n // tk,),
        in_specs=[pl.BlockSpec((tk, D_MODEL), lambda i: (i, 0)),
                  pl.BlockSpec((D_MODEL, PEER_HEADS * D_QUERY), lambda i: (0, 0)),
                  pl.BlockSpec((N_KEYS, D_QUERY // 2), lambda i: (0, 0)),
                  pl.BlockSpec((N_KEYS, D_QUERY // 2), lambda i: (0, 0))],
        out_specs=[ospec, ospec, ospec],
        out_shape=[jax.ShapeDtypeStruct((n, N_PICK), jnp.int32),
                   jax.ShapeDtypeStruct((n, N_PICK), jnp.int32),
                   jax.ShapeDtypeStruct((n, N_PICK), F32)],
        scratch_shapes=[pltpu.VMEM((N_PICK, tk), F32)] * 3,
        compiler_params=pltpu.CompilerParams(dimension_semantics=("arbitrary",)),
        name="retrieve",
    )(h2, wq, k1, k2)


TILE_ROWS = 2 * SUBLANES
IDX_PARTS = 2
KG = N_PICK * TILE_ROWS


def _tile_table(tbl):
    halves = lax.bitcast_convert_type(tbl.astype(BF16), jnp.uint16).astype(jnp.uint32)
    halves = halves.reshape(-1, SUBLANES, 2, LANES)
    words = halves[:, :, 0, :] | (halves[:, :, 1, :] << 16)
    return words.reshape(-1, LANES)


def _gathered_rows(tbl_ref, off_ref, p):
    tiles = []
    for j in range(N_PICK):
        off = pl.multiple_of(off_ref[p, j], SUBLANES)
        tiles.append(pltpu.bitcast(tbl_ref[pl.ds(off, SUBLANES), :], BF16))
    return jnp.concatenate(tiles, axis=0)


def _for_each_token(off_hbm, off_smem, sems, tb, one_token):
    step = pl.program_id(0)
    part = tb // IDX_PARTS

    def copy(block, k):
        rows = pl.ds(k * part, part)
        return pltpu.make_async_copy(off_hbm.at[pl.ds(block * tb + k * part, part)],
                                     off_smem.at[rows], sems.at[k])

    @pl.when(step == 0)
    def _():
        for k in range(IDX_PARTS):
            copy(0, k).start()

    for k in range(IDX_PARTS):
        copy(step, k).wait()
        for p in range(k * part, (k + 1) * part):
            one_token(p)

        @pl.when(step + 1 < pl.num_programs(0))
        def _():
            copy(step + 1, k).start()


def _split_bf16(x):
    hi = x.astype(BF16)
    return hi, (x - hi.astype(F32)).astype(BF16)


def _pick_patterns():
    col = jnp.arange(KG, dtype=jnp.int32)
    pick = jnp.arange(N_PICK, dtype=jnp.int32)
    expand = (col[None, :] // TILE_ROWS == pick[:, None]).astype(BF16)
    half = col[:, None] // SUBLANES
    sum_even = (half == 2 * pick[None, :]).astype(BF16)
    sum_odd = (half == 2 * pick[None, :] + 1).astype(BF16)
    return expand, sum_even, sum_odd


N_MXU = N_PICK // 2
KM = N_MXU * TILE_ROWS


def _peer_act_kernel(tix_hbm, x_ref, par_ref, g_ref, sum_even_ref, sum_odd_ref, tbl_ref, w_ref,
                     xa_scr, zs_scr, r_scr, actv_scr, tix_smem, sems):
    tb = x_ref.shape[0]
    x = x_ref[...]
    xa_scr[...] = jnp.concatenate([x, x - x.astype(BF16).astype(F32)], axis=1).astype(BF16)
    r_scr[...] = jnp.zeros_like(r_scr)
    keep = ((lax.broadcasted_iota(jnp.int32, (SUBLANES, KM), 1) & (SUBLANES - 1))
            == lax.broadcasted_iota(jnp.int32, (SUBLANES, KM), 0))
    nt = (((1,), (1,)), ((), ()))
    half = SUBLANES // 2

    def one_token(p):
        tiles = []
        for j in range(N_MXU):
            off = pl.multiple_of(tix_smem[p, j], SUBLANES)
            tiles.append(pltpu.bitcast(tbl_ref[pl.ds(off, SUBLANES), :], BF16))
        rows = jnp.concatenate(tiles, axis=0)
        res = lax.dot_general(xa_scr[p], rows, nt, preferred_element_type=F32)
        z = jnp.where(keep, res[0:SUBLANES] + res[SUBLANES:], 0.0)
        zs_scr[p:p + 1, :] = jnp.sum(z, axis=0, keepdims=True)

        xp = x_ref[p]
        x_even = jnp.concatenate([xp[c:c + 1] for c in (0, 2, 4, 6)] * 2, axis=0)
        x_odd = jnp.concatenate([xp[c:c + 1] for c in (1, 3, 5, 7)] * 2, axis=0)
        for j in range(N_MXU, N_PICK):
            off = pl.multiple_of(tix_smem[p, j], SUBLANES)
            word = tbl_ref[pl.ds(off, SUBLANES), :]
            rows_even = lax.bitcast_convert_type(word << 16, F32)
            rows_odd = lax.bitcast_convert_type(word & jnp.uint32(0xFFFF0000), F32)
            r_scr[:, j:j + 1] = jnp.sum(rows_even * x_even + rows_odd * x_odd, axis=1, keepdims=True)
        r = r_scr[...]
        first = jnp.sum(r[0:half], axis=0, keepdims=True)
        second = jnp.sum(r[half:], axis=0, keepdims=True)
        actv_scr[p:p + 1, :] = jnp.where(par_ref[p:p + 1, :] == 1, second, first)

    _for_each_token(tix_hbm, tix_smem, sems, tb, one_token)

    sum_even = sum_even_ref[0:KM, :]
    sum_odd = sum_odd_ref[0:KM, :]
    z_hi, z_lo = _split_bf16(zs_scr[...])
    even = (jnp.dot(z_hi, sum_even, preferred_element_type=F32)
            + jnp.dot(z_lo, sum_even, preferred_element_type=F32))
    odd = (jnp.dot(z_hi, sum_odd, preferred_element_type=F32)
           + jnp.dot(z_lo, sum_odd, preferred_element_type=F32))
    act = jnp.where(par_ref[...] == 1, odd, even) + actv_scr[...]
    w_ref[...] = g_ref[...] * jax.nn.gelu(act)


def _peer_act(tix, x3, par, gates, sum_even, sum_odd, tbl, tb):
    n = tix.shape[0]
    vspec = pl.BlockSpec((tb, N_PICK), lambda i: (i, 0))
    whole = pl.BlockSpec(memory_space=pltpu.VMEM)
    return pl.pallas_call(
        _peer_act_kernel,
        grid=(n // tb,),
        in_specs=[pl.BlockSpec(memory_space=pl.ANY),
                  pl.BlockSpec((tb, SUBLANES, LANES), lambda i: (i, 0, 0)),
                  vspec, vspec, whole, whole, whole],
        out_specs=vspec,
        out_shape=jax.ShapeDtypeStruct((n, N_PICK), F32),
        scratch_shapes=[pltpu.VMEM((tb, TILE_ROWS, LANES), BF16), pltpu.VMEM((tb, KM), F32),
                        pltpu.VMEM((SUBLANES, N_PICK), F32), pltpu.VMEM((tb, N_PICK), F32),
                        pltpu.SMEM((tb, N_PICK), jnp.int32),
                        pltpu.SemaphoreType.DMA((IDX_PARTS,))],
        compiler_params=pltpu.CompilerParams(
            dimension_semantics=("arbitrary",), vmem_limit_bytes=PEER_VMEM_LIMIT),
        name="peer_act",
    )(tix, x3, par, gates, sum_even, sum_odd, tbl)


def _peer_out_kernel(tix_hbm, w_ref, par_ref, x1_ref, g2_ref, fw_ref, expand_ref, tbl_ref, y_ref,
                     wexp_scr, sum_scr, tix_smem, sems):
    tb = x1_ref.shape[0]

    expand = expand_ref[...]
    w_hi, w_lo = _split_bf16(w_ref[...])
    par8 = (par_ref[...] * SUBLANES).astype(F32).astype(BF16)
    wexp_scr[0:tb, :] = jnp.dot(w_hi, expand, preferred_element_type=F32)
    wexp_scr[tb:2 * tb, :] = jnp.dot(w_lo, expand, preferred_element_type=F32)
    wexp_scr[2 * tb:3 * tb, :] = jnp.dot(par8, expand, preferred_element_type=F32)

    rel = ((lax.broadcasted_iota(jnp.int32, (SUBLANES, KG), 1) & (TILE_ROWS - 1))
           - lax.broadcasted_iota(jnp.int32, (SUBLANES, KG), 0)).astype(F32)

    def one_token(p):
        sel = rel == wexp_scr[2 * tb + p:2 * tb + p + 1, :]
        a_hi = jnp.where(sel, wexp_scr[p:p + 1, :], 0.0)
        a_lo = jnp.where(sel, wexp_scr[tb + p:tb + p + 1, :], 0.0)
        a = jnp.concatenate([a_hi, a_lo], axis=0).astype(BF16)
        rows = _gathered_rows(tbl_ref, tix_smem, p)
        res = jnp.dot(a, rows, preferred_element_type=F32)
        sum_scr[p] = res[0:SUBLANES] + res[SUBLANES:]

    _for_each_token(tix_hbm, tix_smem, sems, tb, one_token)

    out = jnp.concatenate([sum_scr[:, c, :] for c in range(SUBLANES)], axis=1)
    y_ref[...] = _rms(x1_ref[...] + g2_ref[0] * out, fw_ref[...])


def _peer_out(tix, w, par, x1, g2, fw, expand, tbl, tb, tokens_per_batch):
    n = tix.shape[0]
    vspec = pl.BlockSpec((tb, N_PICK), lambda i: (i, 0))
    xspec = pl.BlockSpec((tb, D_MODEL), lambda i: (i, 0))
    whole = pl.BlockSpec(memory_space=pltpu.VMEM)
    return pl.pallas_call(
        _peer_out_kernel,
        grid=(n // tb,),
        in_specs=[pl.BlockSpec(memory_space=pl.ANY),
                  vspec, vspec, xspec,
                  pl.BlockSpec((1, 1, D_MODEL), lambda i: ((i * tb) // tokens_per_batch, 0, 0)),
                  pl.BlockSpec((1, D_MODEL), lambda i: (0, 0)),
                  whole, whole],
        out_specs=xspec,
        out_shape=jax.ShapeDtypeStruct((n, D_MODEL), F32),
        scratch_shapes=[pltpu.VMEM((3 * tb, KG), F32),
                        pltpu.VMEM((tb, SUBLANES, LANES), F32),
                        pltpu.SMEM((tb, N_PICK), jnp.int32),
                        pltpu.SemaphoreType.DMA((IDX_PARTS,))],
        compiler_params=pltpu.CompilerParams(
            dimension_semantics=("arbitrary",), vmem_limit_bytes=PEER_VMEM_LIMIT),
        name="peer_out",
    )(tix, w, par, x1, g2, fw, expand, tbl)


def _trunk(x, mod, h0, conv0, c0, n0, m0, chunk_len, p):
    B, T, _ = x.shape
    n = B * T
    tm = min(512, T)
    z, zg = _inproj(x, mod, p["norm1_w"], p["w_main"], p["b_main"], p["w_gate"], p["b_gate"], tm)
    ya, h_last, conv_new = _rglru(z, h0.reshape(B, 1, W_A), conv0, p["conv_w"], p["conv_b"],
                                  p["w_ri"], p["b_ri"], p["lam"], tm)
    m0b = jnp.broadcast_to(m0[:, :, None], (B, H_B, LANES))
    yb, c_new, n_new, m_new = _mlstm(z, zg, c0, n0, m0b, p["mlstm_norm_w"], chunk_len, tm, 1)
    x1, h2 = _outproj(x, ya, yb, mod, p["wo_a"], p["wo_b"], p["norm2_w"], tm)

    tk = 256
    tix, par, gates = _retrieve(h2.reshape(n, D_MODEL), p["wq"], p["k1"], p["k2"], tk)
    tb = min(128, T)
    expand, sum_even, sum_odd = _pick_patterns()
    w = _peer_act(tix, h2.reshape(n, SUBLANES, LANES), par, gates, sum_even, sum_odd,
                  p["u_tbl"], tb)
    y = _peer_out(tix, w, par, x1.reshape(n, D_MODEL), mod[:, 5:6], p["final_w"], expand,
                  p["v_tbl"], tb, T)
    return (y.reshape(B, T, D_MODEL), h_last.reshape(B, W_A), conv_new, c_new, n_new,
            m_new[:, :, 0])


def kernel(x_prompt, x_sample, c_prompt, c_sample, state_lru_h, state_lru_conv, state_mlstm_C,
           state_mlstm_n, state_mlstm_m, norm1_w, ada_w, ada_b, w_in, b_in, conv_w, conv_b,
           lru_wr, lru_br, lru_wi, lru_bi, lru_lambda, mlstm_norm_w, w_out, norm2_w,
           peer_wq, peer_k1, peer_k2, peer_u, peer_v, final_norm_w):
    depth = w_in.shape[0]
    assert depth == 1, "single-layer trunk only"
    B = x_prompt.shape[0]
    Bs = x_sample.shape[0]
    l = 0

    def block_diag(w):
        nb, c, d = w.shape
        eye = jnp.eye(nb, dtype=w.dtype)
        return jnp.einsum("ncd,nm->ncmd", w, eye).reshape(nb * c, nb * d)

    pad_g = LANES - N_GATE
    p = {
        "norm1_w": norm1_w[l].reshape(1, D_MODEL),
        "w_main": w_in[l][:, :N_MAIN].astype(BF16),
        "b_main": b_in[l][:N_MAIN].reshape(1, N_MAIN),
        "w_gate": jnp.pad(w_in[l][:, N_MAIN:], ((0, 0), (0, pad_g))),
        "b_gate": jnp.pad(b_in[l][N_MAIN:], (0, pad_g)).reshape(1, LANES),
        "conv_w": conv_w[l],
        "conv_b": conv_b[l].reshape(1, W_A),
        "w_ri": jnp.concatenate([block_diag(lru_wr[l]), block_diag(lru_wi[l])], axis=1).astype(BF16),
        "b_ri": jnp.concatenate([lru_br[l], lru_bi[l]]).reshape(1, 2 * W_A),
        "lam": lru_lambda[l].reshape(1, W_A),
        "mlstm_norm_w": mlstm_norm_w[l].reshape(1, W_B),
        "wo_a": w_out[l][:W_A].astype(BF16),
        "wo_b": w_out[l][W_A:].astype(BF16),
        "norm2_w": norm2_w[l].reshape(1, D_MODEL),
        "wq": peer_wq[l].astype(BF16),
        "k1": peer_k1[l],
        "k2": peer_k2[l],
        "u_tbl": _tile_table(peer_u[l]),
        "v_tbl": _tile_table(peer_v[l]),
        "final_w": final_norm_w.reshape(1, D_MODEL),
    }

    mod = _adaln(jnp.concatenate([c_prompt, c_sample], axis=0), ada_w[l], ada_b[l])
    mod = mod.reshape(B + Bs, N_ADA, D_MODEL)

    dt = x_prompt.dtype
    outs_p = _trunk(x_prompt, mod[:B], jnp.zeros((B, W_A), dt), jnp.zeros((B, CONV_W - 1, W_A), dt),
                    jnp.zeros((B, H_B, DK, DK), dt), jnp.zeros((B, H_B, DK), dt),
                    jnp.zeros((B, H_B), dt), CHUNK, p)
    outs_s = _trunk(x_sample, mod[B:], state_lru_h[l], state_lru_conv[l], state_mlstm_C[l],
                    state_mlstm_n[l], state_mlstm_m[l], x_sample.shape[1], p)
    y_p, *st_p = outs_p
    y_s, *st_s = outs_s
    st_p = [s[None].astype(r.dtype) for s, r in zip(
        st_p, (state_lru_h, state_lru_conv, state_mlstm_C, state_mlstm_n, state_mlstm_m))]
    st_s = [s[None].astype(r.dtype) for s, r in zip(
        st_s, (state_lru_h, state_lru_conv, state_mlstm_C, state_mlstm_n, state_mlstm_m))]
    return (y_p, y_s, *st_p, *st_s)
```
